```python
import jax, jax.numpy as jnp
from jax import lax
import numpy as np

D_MODEL = 2048
BATCH = 4
SEQ = 4096
DEPTH = 2

CHUNK = 64
N_META = 16
D_LRU = D_MODEL // 2
D_CONV = D_MODEL // 2
D_MIX = D_LRU + D_CONV
LRU_HEADS = 16
LRU_HEAD_DIM = D_LRU // LRU_HEADS
CONV_GROUPS = 16
LRU_CONV_W = 4
SHORT_CONV_W = 3
LRU_C = 8.0
RMS_EPS = 1e-6
SPLIT_SIZES = (D_LRU, D_LRU, D_CONV, D_CONV, D_CONV, D_CONV)
D_IN = sum(SPLIT_SIZES)
SPLIT_IDX = tuple(int(v) for v in np.cumsum(SPLIT_SIZES)[:-1])

kernel_name = "hymba_rglru_shortconv_trunk"


def rmsnorm(x, g):
    xf = x.astype(jnp.float32)
    y = xf * lax.rsqrt(jnp.mean(xf * xf, axis=-1, keepdims=True) + RMS_EPS)
    return (y * g.astype(jnp.float32)).astype(x.dtype)


def causal_depthwise_conv(x, w):
    k, c = w.shape
    return lax.conv_general_dilated(
        x, w[:, None, :].astype(x.dtype), window_strides=(1,),
        padding=((k - 1, 0),), dimension_numbers=("NWC", "WIO", "NWC"),
        feature_group_count=c)


def rg_lru(x, wr, br, wi, bi, lam):
    bsz, length, _ = x.shape
    xf = x.astype(jnp.float32)
    xh = xf.reshape(bsz, length, LRU_HEADS, LRU_HEAD_DIM)
    r = jax.nn.sigmoid(jnp.einsum("blhi,hij->blhj", xh, wr.astype(jnp.float32))
                       .reshape(bsz, length, D_LRU) + br.astype(jnp.float32))
    i = jax.nn.sigmoid(jnp.einsum("blhi,hij->blhj", xh, wi.astype(jnp.float32))
                       .reshape(bsz, length, D_LRU) + bi.astype(jnp.float32))
    log_a = -LRU_C * r * jax.nn.softplus(-lam.astype(jnp.float32))
    a = jnp.exp(log_a)
    b = jnp.sqrt(-jnp.expm1(2.0 * log_a)) * (i * xf)

    def combine(left, right):
        a1, b1 = left
        a2, b2 = right
        return a1 * a2, a2 * b1 + b2

    _, h = lax.associative_scan(combine, (a, b), axis=1)
    return h.astype(x.dtype)


def hybrid_layer(x, norm_g, w_in, conv_a_w, conv_a_b, lru_wr, lru_br, lru_wi,
                 lru_bi, lru_lambda, conv_b_w, w_out):
    h = rmsnorm(x, norm_g)
    u = jnp.einsum("bld,de->ble", h, w_in.astype(h.dtype))
    xa, ga, gate_b, gate_c, xb, gb = jnp.split(u, SPLIT_IDX, axis=-1)
    xa = causal_depthwise_conv(xa, conv_a_w) + conv_a_b.astype(xa.dtype)
    ya = rg_lru(xa, lru_wr, lru_br, lru_wi, lru_bi, lru_lambda) * jax.nn.silu(ga)
    yb = gate_b * causal_depthwise_conv(gate_c * xb, conv_b_w) * jax.nn.silu(gb)
    y = jnp.concatenate([ya, yb], axis=-1)
    return x + jnp.einsum("ble,ed->bld", y, w_out.astype(y.dtype))


def setup_inputs(seed: int = 0) -> dict:
    key = jax.random.key(seed)
    ks = jax.random.split(key, 16)
    f32 = jnp.float32
    x = jax.random.normal(ks[0], (BATCH, SEQ, D_MODEL), f32)
    meta = jax.random.normal(ks[1], (N_META, D_MODEL), f32)
    norm_g = 1.0 + 0.01 * jax.random.normal(ks[2], (DEPTH, D_MODEL), f32)
    w_in = jax.random.normal(ks[3], (DEPTH, D_MODEL, D_IN), f32) * D_MODEL ** -0.5
    conv_a_w = jax.random.normal(ks[4], (DEPTH, LRU_CONV_W, D_LRU), f32) * LRU_CONV_W ** -0.5
    conv_a_b = 0.01 * jax.random.normal(ks[5], (DEPTH, D_LRU), f32)
    lru_wr = jax.random.normal(ks[6], (DEPTH, LRU_HEADS, LRU_HEAD_DIM, LRU_HEAD_DIM), f32) * LRU_HEAD_DIM ** -0.5
    lru_br = 0.01 * jax.random.normal(ks[7], (DEPTH, D_LRU), f32)
    lru_wi = jax.random.normal(ks[8], (DEPTH, LRU_HEADS, LRU_HEAD_DIM, LRU_HEAD_DIM), f32) * LRU_HEAD_DIM ** -0.5
    lru_bi = 0.01 * jax.random.normal(ks[9], (DEPTH, D_LRU), f32)
    a_c = jax.random.uniform(ks[10], (DEPTH, D_LRU), f32, 0.9, 0.999)
    a0 = a_c ** (1.0 / LRU_C)
    lru_lambda = jnp.log(a0) - jnp.log1p(-a0)
    conv_b_w = jax.random.normal(ks[11], (DEPTH, SHORT_CONV_W, D_CONV), f32) * SHORT_CONV_W ** -0.5
    w_out = jax.random.normal(ks[12], (DEPTH, D_MIX, D_MODEL), f32) * D_MIX ** -0.5
    final_g = 1.0 + 0.01 * jax.random.normal(ks[13], (D_MODEL,), f32)
    return {"x": x, "meta": meta, "norm_g": norm_g, "w_in": w_in,
            "conv_a_w": conv_a_w, "conv_a_b": conv_a_b, "lru_wr": lru_wr,
            "lru_br": lru_br, "lru_wi": lru_wi, "lru_bi": lru_bi,
            "lru_lambda": lru_lambda, "conv_b_w": conv_b_w, "w_out": w_out,
            "final_g": final_g}


def reference(x, meta, norm_g, w_in, conv_a_w, conv_a_b, lru_wr, lru_br, lru_wi,
              lru_bi, lru_lambda, conv_b_w, w_out, final_g):
    bsz = x.shape[0]
    m = jnp.broadcast_to(meta.astype(x.dtype)[None], (bsz, N_META, D_MODEL))
    h = jnp.concatenate([m, x], axis=1)
    for layer in range(DEPTH):
        h = hybrid_layer(h, norm_g[layer], w_in[layer], conv_a_w[layer],
                         conv_a_b[layer], lru_wr[layer], lru_br[layer],
                         lru_wi[layer], lru_bi[layer], lru_lambda[layer],
                         conv_b_w[layer], w_out[layer])
    return rmsnorm(h[:, N_META:], final_g)
```

```python
import functools

import jax
import jax.numpy as jnp
from jax import lax
from jax.experimental import pallas as pl
from jax.experimental.pallas import tpu as pltpu

LRU_C = 8.0
RMS_EPS = 1e-6
N_SPLITS = 6
LRU_CONV_TAPS = 4
SHORT_CONV_TAPS = 3

SUBLANES = 8
CHANNEL_BLOCK = 256
TAIL_ROWS = SUBLANES
MAIN_TILE_ROWS = 256
VMEM_LIMIT_BYTES = 58 * 1024 * 1024


def _sigmoid(x):
    return 1.0 / (1.0 + jnp.exp(-x))


def _softplus(x):
    return jnp.maximum(x, 0.0) + jnp.log1p(jnp.exp(-jnp.abs(x)))


def _one_minus_square_of_exp(log_a, a):
    y = 2.0 * log_a
    u = a * a
    near_zero = jnp.where(u == 1.0, -y, (1.0 - u) * y / jnp.log(u))
    return jnp.where(y < -0.5, 1.0 - u, near_zero)


def _rmsnorm(x, g):
    return x * lax.rsqrt(jnp.mean(x * x, axis=-1, keepdims=True) + RMS_EPS) * g


def _scan_rows(a, b, carry):
    rows, width = a.shape
    blocks = rows // SUBLANES
    a3 = a.reshape(blocks, SUBLANES, width)
    b3 = b.reshape(blocks, SUBLANES, width)
    row = lax.broadcasted_iota(jnp.int32, (blocks, SUBLANES, width), 1)
    shift = 1
    while shift < SUBLANES:
        keep = row >= shift
        a_prev = pltpu.roll(a3, shift, axis=1)
        b_prev = pltpu.roll(b3, shift, axis=1)
        b3 = jnp.where(keep, a3 * b_prev + b3, b3)
        a3 = jnp.where(keep, a3 * a_prev, a3)
        shift *= 2
    out = []
    for k in range(blocks):
        hk = a3[k] * carry + b3[k]
        out.append(hk)
        carry = hk[SUBLANES - 1:SUBLANES, :]
    return jnp.concatenate(out, axis=0), carry


def _layer_kernel(x_ref, g_ref, win_ref, caw_ref, cab_ref, wr_ref, br_ref, wi_ref, bi_ref,
                  lam_ref, cbw_ref, wout_ref, fg_ref, sa0_ref, sb0_ref, h0_ref,
                  *rest, rows, final_norm, emit_state):
    if emit_state:
        out_ref, sa_out, sb_out, h_out, xa_buf, cb_buf, h_carry, hn_buf, y_buf = rest
    else:
        out_ref, xa_buf, cb_buf, h_carry, hn_buf, y_buf = rest
    chans = lam_ref.shape[-1]
    f32, bf16 = jnp.float32, jnp.bfloat16

    @pl.when(pl.program_id(1) == 0)
    def _():
        xa_buf[0:TAIL_ROWS, :] = sa0_ref[...]
        cb_buf[0:TAIL_ROWS, :] = sb0_ref[...]
        h_carry[...] = h0_ref[...]

    x = x_ref[...]
    hn_buf[...] = _rmsnorm(x, g_ref[...]).astype(bf16)

    def proj(split, cs):
        col = split * chans + cs.start
        return jnp.dot(hn_buf[...], win_ref[:, col:col + CHANNEL_BLOCK], preferred_element_type=f32)

    def shifted(buf, back, cs):
        return buf[TAIL_ROWS - back:TAIL_ROWS - back + rows, cs]

    for c in range(chans // CHANNEL_BLOCK):
        cs = slice(c * CHANNEL_BLOCK, (c + 1) * CHANNEL_BLOCK)

        xa = proj(0, cs)
        xa_buf[TAIL_ROWS:TAIL_ROWS + rows, cs] = xa
        xc = cab_ref[:, cs] + caw_ref[LRU_CONV_TAPS - 1:LRU_CONV_TAPS, cs] * xa
        for k in range(LRU_CONV_TAPS - 1):
            xc = xc + caw_ref[k:k + 1, cs] * shifted(xa_buf, LRU_CONV_TAPS - 1 - k, cs)
        xcb = xc.astype(bf16)
        r = _sigmoid(jnp.dot(xcb, wr_ref[c], preferred_element_type=f32) + br_ref[:, cs])
        i = _sigmoid(jnp.dot(xcb, wi_ref[c], preferred_element_type=f32) + bi_ref[:, cs])
        log_a = (-LRU_C) * r * _softplus(-lam_ref[:, cs])
        a = jnp.exp(log_a)
        b = jnp.sqrt(_one_minus_square_of_exp(log_a, a)) * (i * xc)
        h, carry = _scan_rows(a, b, h_carry[:, cs])
        h_carry[:, cs] = carry
        ga = proj(1, cs)
        y_buf[:, cs] = (h * (ga * _sigmoid(ga))).astype(bf16)

        gate_b = proj(2, cs)
        cb = proj(3, cs) * proj(4, cs)
        cb_buf[TAIL_ROWS:TAIL_ROWS + rows, cs] = cb
        cv = cbw_ref[SHORT_CONV_TAPS - 1:SHORT_CONV_TAPS, cs] * cb
        for k in range(SHORT_CONV_TAPS - 1):
            cv = cv + cbw_ref[k:k + 1, cs] * shifted(cb_buf, SHORT_CONV_TAPS - 1 - k, cs)
        gb = proj(5, cs)
        y_buf[:, chans + c * CHANNEL_BLOCK:chans + (c + 1) * CHANNEL_BLOCK] = (
            gate_b * cv * (gb * _sigmoid(gb))).astype(bf16)

    xa_buf[0:TAIL_ROWS, :] = xa_buf[rows:rows + TAIL_ROWS, :]
    cb_buf[0:TAIL_ROWS, :] = cb_buf[rows:rows + TAIL_ROWS, :]

    out = x + jnp.dot(y_buf[...], wout_ref[...], preferred_element_type=f32)
    if final_norm:
        out = _rmsnorm(out, fg_ref[...])
    out_ref[...] = out

    if emit_state:
        sa_out[...] = xa_buf[0:TAIL_ROWS, :]
        sb_out[...] = cb_buf[0:TAIL_ROWS, :]
        h_out[...] = h_carry[...]


def _resident(shape):
    zeros = (0,) * len(shape)
    return pl.BlockSpec(shape, lambda b, t: zeros, pipeline_mode=pl.Buffered(1))


def _layer_call(x, p, state, *, rows, final_norm, emit_state, name):
    batch, length, d_model = x.shape
    chans = p["lam"].shape[-1]
    assert length % rows == 0 and rows % SUBLANES == 0 and chans % CHANNEL_BLOCK == 0
    f32 = jnp.float32
    params = [p["g"], p["win"], p["caw"], p["cab"], p["wr"], p["br"], p["wi"], p["bi"],
              p["lam"], p["cbw"], p["wout"], p["fg"]]
    tile = pl.BlockSpec((None, rows, d_model), lambda b, t: (b, t, 0))
    in_specs = [tile] + [_resident(a.shape) for a in params + list(state)]
    out_shape = [jax.ShapeDtypeStruct(x.shape, f32)]
    out_specs = [tile]
    if emit_state:
        out_shape += [jax.ShapeDtypeStruct(s.shape, f32) for s in state]
        out_specs += [pl.BlockSpec(s.shape, lambda b, t: (0, 0)) for s in state]
    scratch = [
        pltpu.VMEM((TAIL_ROWS + rows, chans), f32),
        pltpu.VMEM((TAIL_ROWS + rows, chans), f32),
        pltpu.VMEM((1, chans), f32),
        pltpu.VMEM((rows, d_model), jnp.bfloat16),
        pltpu.VMEM((rows, 2 * chans), jnp.bfloat16),
    ]
    kern = functools.partial(_layer_kernel, rows=rows, final_norm=final_norm, emit_state=emit_state)
    return pl.pallas_call(
        kern,
        grid=(batch, length // rows),
        in_specs=in_specs,
        out_specs=out_specs,
        out_shape=out_shape,
        scratch_shapes=scratch,
        compiler_params=pltpu.CompilerParams(
            dimension_semantics=("arbitrary", "arbitrary"),
            vmem_limit_bytes=VMEM_LIMIT_BYTES),
        name=name,
    )(x, *params, *state)


def _block_diag(w, group):
    heads, d, _ = w.shape
    w = w.reshape(heads // group, group, d, d)
    eye = jnp.eye(group, dtype=w.dtype)
    return jnp.einsum("ngij,gh->ngihj", w, eye).reshape(heads // group, group * d, group * d)


def kernel(x, meta, norm_g, w_in, conv_a_w, conv_a_b, lru_wr, lru_br, lru_wi, lru_bi, lru_lambda,
           conv_b_w, w_out, final_g):
    depth = norm_g.shape[0]
    chans = lru_lambda.shape[-1]
    head_dim = lru_wr.shape[-1]
    assert w_in.shape[-1] == N_SPLITS * chans
    bf16 = jnp.bfloat16
    group = CHANNEL_BLOCK // head_dim
    zero_state = (jnp.zeros((TAIL_ROWS, chans), jnp.float32),
                  jnp.zeros((TAIL_ROWS, chans), jnp.float32),
                  jnp.zeros((1, chans), jnp.float32))
    h_meta = meta[None]
    h = x
    for layer in range(depth):
        p = dict(
            g=norm_g[layer][None], win=w_in[layer].astype(bf16),
            caw=conv_a_w[layer], cab=conv_a_b[layer][None],
            wr=_block_diag(lru_wr[layer], group).astype(bf16), br=lru_br[layer][None],
            wi=_block_diag(lru_wi[layer], group).astype(bf16), bi=lru_bi[layer][None],
            lam=lru_lambda[layer][None], cbw=conv_b_w[layer],
            wout=w_out[layer].astype(bf16), fg=final_g[None])
        last = layer == depth - 1
        h_meta, *state = _layer_call(h_meta, p, zero_state, rows=meta.shape[0], final_norm=False,
                                     emit_state=True, name=f"meta_layer{layer}")
        (h,) = _layer_call(h, p, state, rows=MAIN_TILE_ROWS, final_norm=last,
                           emit_state=False, name=f"main_layer{layer}")
    return h
```

```python
import functools

import jax
import jax.numpy as jnp
from jax import lax
from jax.experimental import pallas as pl
from jax.experimental.pallas import tpu as pltpu

LRU_C = 8.0
RMS_EPS = 1e-6
N_SPLITS = 6
LRU_CONV_TAPS = 4
SHORT_CONV_TAPS = 3

SUBLANES = 8
CHANNEL_BLOCK = 256
TAIL_ROWS = SUBLANES
MAIN_TILE_ROWS = 256
VMEM_LIMIT_BYTES = 58 * 1024 * 1024


def _sigmoid(x):
    return 1.0 / (1.0 + jnp.exp(-x))


def _softplus(x):
    return jnp.maximum(x, 0.0) + jnp.log1p(jnp.exp(-jnp.abs(x)))


def _one_minus_square_of_exp(log_a, a):
    y = 2.0 * log_a
    u = a * a
    near_zero = jnp.where(u == 1.0, -y, (1.0 - u) * y / jnp.log(u))
    return jnp.where(y < -0.5, 1.0 - u, near_zero)


def _as_bf16(packed):
    return pltpu.bitcast(packed, jnp.bfloat16)


def _rmsnorm(x, g):
    return x * lax.rsqrt(jnp.mean(x * x, axis=-1, keepdims=True) + RMS_EPS) * g


def _scan_rows(a, b, carry):
    rows, width = a.shape
    blocks = rows // SUBLANES
    a3 = a.reshape(blocks, SUBLANES, width)
    b3 = b.reshape(blocks, SUBLANES, width)
    row = lax.broadcasted_iota(jnp.int32, (blocks, SUBLANES, width), 1)
    shift = 1
    while shift < SUBLANES:
        keep = row >= shift
        a_prev = pltpu.roll(a3, shift, axis=1)
        b_prev = pltpu.roll(b3, shift, axis=1)
        b3 = jnp.where(keep, a3 * b_prev + b3, b3)
        a3 = jnp.where(keep, a3 * a_prev, a3)
        shift *= 2
    last = slice(SUBLANES - 1, SUBLANES)
    carry = jnp.broadcast_to(carry, (SUBLANES, width))
    out = []
    for k in range(blocks):
        out.append(a3[k] * carry + b3[k])
        carry = (jnp.broadcast_to(a3[k][last], (SUBLANES, width)) * carry
                 + jnp.broadcast_to(b3[k][last], (SUBLANES, width)))
    return jnp.concatenate(out, axis=0), carry[last]


def _layer_kernel(x_ref, g_ref, win_ref, caw_ref, cab_ref, wr_ref, br_ref, wi_ref, bi_ref,
                  lam_ref, cbw_ref, wout_ref, fg_ref, sa0_ref, sb0_ref, h0_ref,
                  *rest, rows, final_norm, emit_state):
    if emit_state:
        out_ref, sa_out, sb_out, h_out, xa_buf, cb_buf, h_carry, hn_buf, y_buf = rest
    else:
        out_ref, xa_buf, cb_buf, h_carry, hn_buf, y_buf = rest
    chans = lam_ref.shape[-1]
    f32, bf16 = jnp.float32, jnp.bfloat16

    @pl.when(pl.program_id(1) == 0)
    def _():
        xa_buf[0:TAIL_ROWS, :] = sa0_ref[...]
        cb_buf[0:TAIL_ROWS, :] = sb0_ref[...]
        h_carry[...] = h0_ref[...]

    x = x_ref[...]
    hn = _rmsnorm(x, g_ref[...]).astype(bf16)

    def proj(split, c):
        col = split * chans + c * CHANNEL_BLOCK
        return jnp.dot(hn, _as_bf16(win_ref[:, col:col + CHANNEL_BLOCK]), preferred_element_type=f32)

    def shifted(buf, back, cs):
        return buf[TAIL_ROWS - back:TAIL_ROWS - back + rows, cs]

    for c in range(chans // CHANNEL_BLOCK):
        cs = slice(c * CHANNEL_BLOCK, (c + 1) * CHANNEL_BLOCK)
        ys = slice(chans + c * CHANNEL_BLOCK, chans + (c + 1) * CHANNEL_BLOCK)

        xa = proj(0, c)
        xa_buf[TAIL_ROWS:TAIL_ROWS + rows, cs] = xa
        xc = cab_ref[:, cs] + caw_ref[LRU_CONV_TAPS - 1:LRU_CONV_TAPS, cs] * xa
        for k in range(LRU_CONV_TAPS - 1):
            xc = xc + caw_ref[k:k + 1, cs] * shifted(xa_buf, LRU_CONV_TAPS - 1 - k, cs)
        xcb = xc.astype(bf16)
        r = _sigmoid(jnp.dot(xcb, _as_bf16(wr_ref[c]), preferred_element_type=f32) + br_ref[:, cs])
        i = _sigmoid(jnp.dot(xcb, _as_bf16(wi_ref[c]), preferred_element_type=f32) + bi_ref[:, cs])
        log_a = (-LRU_C) * r * _softplus(-lam_ref[:, cs])
        a = jnp.exp(log_a)
        b = jnp.sqrt(_one_minus_square_of_exp(log_a, a)) * (i * xc)
        h, carry = _scan_rows(a, b, h_carry[:, cs])
        h_carry[:, cs] = carry
        ga = proj(1, c)
        y_buf[:, cs] = (h * (ga * _sigmoid(ga))).astype(bf16)

        gate_b = proj(2, c)
        cb = proj(3, c) * proj(4, c)
        cb_buf[TAIL_ROWS:TAIL_ROWS + rows, cs] = cb
        cv = cbw_ref[SHORT_CONV_TAPS - 1:SHORT_CONV_TAPS, cs] * cb
        for k in range(SHORT_CONV_TAPS - 1):
            cv = cv + cbw_ref[k:k + 1, cs] * shifted(cb_buf, SHORT_CONV_TAPS - 1 - k, cs)
        gb = proj(5, c)
        y_buf[:, ys] = (gate_b * cv * (gb * _sigmoid(gb))).astype(bf16)

    xa_buf[0:TAIL_ROWS, :] = xa_buf[rows:rows + TAIL_ROWS, :]
    cb_buf[0:TAIL_ROWS, :] = cb_buf[rows:rows + TAIL_ROWS, :]

    out = x + jnp.dot(y_buf[...], _as_bf16(wout_ref[...]), preferred_element_type=f32)
    if final_norm:
        out = _rmsnorm(out, fg_ref[...])
    out_ref[...] = out

    if emit_state:
        sa_out[...] = xa_buf[0:TAIL_ROWS, :]
        sb_out[...] = cb_buf[0:TAIL_ROWS, :]
        h_out[...] = h_carry[...]


def _resident(shape, layer=None):
    if layer is None:
        zeros = (0,) * len(shape)
        return pl.BlockSpec(shape, lambda b, t: zeros, pipeline_mode=pl.Buffered(1))
    index = (layer,) + (0,) * (len(shape) - 1)
    return pl.BlockSpec((None,) + tuple(shape[1:]), lambda b, t: index, pipeline_mode=pl.Buffered(1))


def _layer_call(x, stacks, fg, state, layer, *, rows, final_norm, emit_state, name):
    batch, length, d_model = x.shape
    chans = state[0].shape[-1]
    assert length % rows == 0 and rows % SUBLANES == 0 and chans % CHANNEL_BLOCK == 0
    f32 = jnp.float32
    tile = pl.BlockSpec((None, rows, d_model), lambda b, t: (b, t, 0))
    in_specs = ([tile] + [_resident(a.shape, layer) for a in stacks]
                + [_resident(a.shape) for a in (fg,) + tuple(state)])
    out_shape = [jax.ShapeDtypeStruct(x.shape, f32)]
    out_specs = [tile]
    if emit_state:
        out_shape += [jax.ShapeDtypeStruct(s.shape, f32) for s in state]
        out_specs += [pl.BlockSpec(s.shape, lambda b, t: (0, 0)) for s in state]
    scratch = [
        pltpu.VMEM((TAIL_ROWS + rows, chans), f32),
        pltpu.VMEM((TAIL_ROWS + rows, chans), f32),
        pltpu.VMEM((1, chans), f32),
        pltpu.VMEM((rows, d_model), jnp.bfloat16),
        pltpu.VMEM((rows, 2 * chans), jnp.bfloat16),
    ]
    kern = functools.partial(_layer_kernel, rows=rows, final_norm=final_norm, emit_state=emit_state)
    return pl.pallas_call(
        kern,
        grid=(batch, length // rows),
        in_specs=in_specs,
        out_specs=out_specs,
        out_shape=out_shape,
        scratch_shapes=scratch,
        compiler_params=pltpu.CompilerParams(
            dimension_semantics=("arbitrary", "arbitrary"),
            vmem_limit_bytes=VMEM_LIMIT_BYTES),
        name=name,
    )(x, *stacks, fg, *state)


def _pack_bf16_rows(w):
    *lead, rows, cols = w.shape
    pairs = w.astype(jnp.bfloat16).reshape(*lead, rows // 2, 2, cols)
    return lax.bitcast_convert_type(jnp.swapaxes(pairs, -1, -2), jnp.uint32)


def _block_diag(w, group):
    depth, heads, d, _ = w.shape
    w = w.reshape(depth, heads // group, group, d, d)
    eye = jnp.eye(group, dtype=w.dtype)
    return jnp.einsum("lngij,gh->lngihj", w, eye).reshape(depth, heads // group, group * d, group * d)


def kernel(x, meta, norm_g, w_in, conv_a_w, conv_a_b, lru_wr, lru_br, lru_wi, lru_bi, lru_lambda,
           conv_b_w, w_out, final_g):
    depth = norm_g.shape[0]
    chans = lru_lambda.shape[-1]
    head_dim = lru_wr.shape[-1]
    assert w_in.shape[-1] == N_SPLITS * chans
    bf16 = jnp.bfloat16
    group = CHANNEL_BLOCK // head_dim
    stacks = (norm_g[:, None], _pack_bf16_rows(w_in), conv_a_w, conv_a_b[:, None],
              _pack_bf16_rows(_block_diag(lru_wr, group)), lru_br[:, None],
              _pack_bf16_rows(_block_diag(lru_wi, group)), lru_bi[:, None],
              lru_lambda[:, None], conv_b_w, _pack_bf16_rows(w_out))
    fg = final_g[None]
    zero_state = (jnp.zeros((TAIL_ROWS, chans), jnp.float32),
                  jnp.zeros((TAIL_ROWS, chans), jnp.float32),
                  jnp.zeros((1, chans), jnp.float32))
    h_meta = meta[None]
    h = x
    for layer in range(depth):
        last = layer == depth - 1
        h_meta, *state = _layer_call(h_meta, stacks, fg, zero_state, layer, rows=meta.shape[0],
                                     final_norm=False, emit_state=True, name=f"meta_layer{layer}")
        (h,) = _layer_call(h, stacks, fg, state, layer, rows=MAIN_TILE_ROWS, final_norm=last,
                           emit_state=False, name=f"main_layer{layer}")
    return h
```

```python
import functools

import jax
import jax.numpy as jnp
from jax import lax
from jax.experimental import pallas as pl
from jax.experimental.pallas import tpu as pltpu

LRU_C = 8.0
RMS_EPS = 1e-6
N_SPLITS = 6
LRU_CONV_TAPS = 4
SHORT_CONV_TAPS = 3

SUBLANES = 8
CHANNEL_BLOCK = 256
TAIL_ROWS = SUBLANES
MAIN_TILE_ROWS = 512
PACK_BLOCK_ROWS = 256
VMEM_LIMIT_BYTES = 60 * 1024 * 1024


NEG_LOG2_E = -1.4426950408889634


def _sigmoid(x):
    return 1.0 / (1.0 + jnp.exp2(x * NEG_LOG2_E))


def _softplus(x):
    return jnp.maximum(x, 0.0) + jnp.log1p(jnp.exp(-jnp.abs(x)))


def _one_minus_square_of_exp(log_a, a):
    y = 2.0 * log_a
    u = a * a
    near_zero = jnp.where(u == 1.0, -y, (1.0 - u) * y / jnp.log(u))
    return jnp.where(y < -0.5, 1.0 - u, near_zero)


def _as_bf16(packed):
    return pltpu.bitcast(packed, jnp.bfloat16)


def _rmsnorm(x, g):
    return x * lax.rsqrt(jnp.mean(x * x, axis=-1, keepdims=True) + RMS_EPS) * g


def _scan_rows(a, b, carry):
    rows, width = a.shape
    blocks = rows // SUBLANES
    a3 = a.reshape(blocks, SUBLANES, width)
    b3 = b.reshape(blocks, SUBLANES, width)
    row = lax.broadcasted_iota(jnp.int32, (blocks, SUBLANES, width), 1)
    shift = 1
    while shift < SUBLANES:
        keep = row >= shift
        a_prev = pltpu.roll(a3, shift, axis=1)
        b_prev = pltpu.roll(b3, shift, axis=1)
        b3 = jnp.where(keep, a3 * b_prev + b3, b3)
        a3 = jnp.where(keep, a3 * a_prev, a3)
        shift *= 2
    last = slice(SUBLANES - 1, SUBLANES)
    carry = jnp.broadcast_to(carry, (SUBLANES, width))
    out = []
    for k in range(blocks):
        out.append(a3[k] * carry + b3[k])
        carry = (jnp.broadcast_to(a3[k][last], (SUBLANES, width)) * carry
                 + jnp.broadcast_to(b3[k][last], (SUBLANES, width)))
    return jnp.concatenate(out, axis=0), carry[last]


def _layer_kernel(x_ref, g_ref, win_ref, caw_ref, cab_ref, wr_ref, br_ref, wi_ref, bi_ref,
                  lam_ref, cbw_ref, wout_ref, fg_ref, sa0_ref, sb0_ref, h0_ref,
                  *rest, rows, final_norm, emit_state):
    if emit_state:
        out_ref, sa_out, sb_out, h_out, xa_tail, cb_tail, h_carry, y_buf = rest
    else:
        out_ref, xa_tail, cb_tail, h_carry, y_buf = rest
    chans = lam_ref.shape[-1]
    f32, bf16 = jnp.float32, jnp.bfloat16

    @pl.when(pl.program_id(1) == 0)
    def _():
        xa_tail[...] = sa0_ref[...]
        cb_tail[...] = sb0_ref[...]
        h_carry[...] = h0_ref[...]

    x = x_ref[...]
    hn = _rmsnorm(x, g_ref[...]).astype(bf16)

    def proj(split, c):
        col = split * chans + c * CHANNEL_BLOCK
        return jnp.dot(hn, _as_bf16(win_ref[:, col:col + CHANNEL_BLOCK]), preferred_element_type=f32)

    blocks = rows // SUBLANES

    def as_blocks(v):
        return v.reshape(blocks, SUBLANES, v.shape[-1])

    def delayed(v3, tail, back):
        rolled = pltpu.roll(jnp.concatenate([tail[None], v3], axis=0), back, axis=1)
        row = lax.broadcasted_iota(jnp.int32, v3.shape, 1)
        return jnp.where(row < back, rolled[:-1], rolled[1:])

    def causal_conv(v3, tail, taps_ref, cs):
        n_taps = taps_ref.shape[0]
        acc = taps_ref[n_taps - 1:n_taps, cs] * v3
        for k in range(n_taps - 1):
            acc = acc + taps_ref[k:k + 1, cs] * delayed(v3, tail, n_taps - 1 - k)
        return acc

    for c in range(chans // CHANNEL_BLOCK):
        cs = slice(c * CHANNEL_BLOCK, (c + 1) * CHANNEL_BLOCK)
        ys = slice(chans + c * CHANNEL_BLOCK, chans + (c + 1) * CHANNEL_BLOCK)

        xa = as_blocks(proj(0, c))
        xc = causal_conv(xa, xa_tail[:, cs], caw_ref, cs) + cab_ref[:, cs]
        xa_tail[:, cs] = xa[blocks - 1]
        xcb = xc.reshape(rows, CHANNEL_BLOCK).astype(bf16)
        r = _sigmoid(jnp.dot(xcb, _as_bf16(wr_ref[c]), preferred_element_type=f32) + br_ref[:, cs])
        i = _sigmoid(jnp.dot(xcb, _as_bf16(wi_ref[c]), preferred_element_type=f32) + bi_ref[:, cs])
        log_a = (-LRU_C) * r * _softplus(-lam_ref[:, cs])
        a = jnp.exp(log_a)
        b = jnp.sqrt(_one_minus_square_of_exp(log_a, a)) * (i * xc.reshape(rows, CHANNEL_BLOCK))
        h, carry = _scan_rows(a, b, h_carry[:, cs])
        h_carry[:, cs] = carry
        ga = proj(1, c)
        y_buf[:, cs] = (h * (ga * _sigmoid(ga))).astype(bf16)

        gate_b = proj(2, c)
        cb = as_blocks(proj(3, c) * proj(4, c))
        cv = causal_conv(cb, cb_tail[:, cs], cbw_ref, cs).reshape(rows, CHANNEL_BLOCK)
        cb_tail[:, cs] = cb[blocks - 1]
        gb = proj(5, c)
        y_buf[:, ys] = (gate_b * cv * (gb * _sigmoid(gb))).astype(bf16)

    out = x + jnp.dot(y_buf[...], _as_bf16(wout_ref[...]), preferred_element_type=f32)
    if final_norm:
        out = _rmsnorm(out, fg_ref[...])
    out_ref[...] = out

    if emit_state:
        sa_out[...] = xa_tail[...]
        sb_out[...] = cb_tail[...]
        h_out[...] = h_carry[...]


def _resident(shape, layer=None):
    if layer is None:
        zeros = (0,) * len(shape)
        return pl.BlockSpec(shape, lambda b, t: zeros, pipeline_mode=pl.Buffered(1))
    index = (layer,) + (0,) * (len(shape) - 1)
    return pl.BlockSpec((None,) + tuple(shape[1:]), lambda b, t: index, pipeline_mode=pl.Buffered(1))


def _layer_call(x, stacks, fg, state, layer, *, rows, final_norm, emit_state, name):
    batch, length, d_model = x.shape
    chans = state[0].shape[-1]
    assert length % rows == 0 and rows % SUBLANES == 0 and chans % CHANNEL_BLOCK == 0
    f32 = jnp.float32
    tile = pl.BlockSpec((None, rows, d_model), lambda b, t: (b, t, 0))
    in_specs = ([tile] + [_resident(a.shape, layer) for a in stacks]
                + [_resident(a.shape) for a in (fg,) + tuple(state)])
    out_shape = [jax.ShapeDtypeStruct(x.shape, f32)]
    out_specs = [tile]
    if emit_state:
        out_shape += [jax.ShapeDtypeStruct(s.shape, f32) for s in state]
        out_specs += [pl.BlockSpec(s.shape, lambda b, t: (0, 0)) for s in state]
    scratch = [
        pltpu.VMEM((TAIL_ROWS, chans), f32),
        pltpu.VMEM((TAIL_ROWS, chans), f32),
        pltpu.VMEM((1, chans), f32),
        pltpu.VMEM((rows, 2 * chans), jnp.bfloat16),
    ]
    kern = functools.partial(_layer_kernel, rows=rows, final_norm=final_norm, emit_state=emit_state)
    return pl.pallas_call(
        kern,
        grid=(batch, length // rows),
        in_specs=in_specs,
        out_specs=out_specs,
        out_shape=out_shape,
        scratch_shapes=scratch,
        compiler_params=pltpu.CompilerParams(
            dimension_semantics=("arbitrary", "arbitrary"),
            vmem_limit_bytes=VMEM_LIMIT_BYTES),
        name=name,
    )(x, *stacks, fg, *state)


def _pack_kernel(w_ref, o_ref):
    o_ref[...] = pltpu.bitcast(w_ref[...].astype(jnp.bfloat16), jnp.uint32)


def _pack_bf16_rows(w, block_rows, name):
    n, rows, cols = w.shape
    assert rows % block_rows == 0 and block_rows % (2 * SUBLANES) == 0
    return pl.pallas_call(
        _pack_kernel,
        grid=(n, rows // block_rows),
        in_specs=[pl.BlockSpec((None, block_rows, cols), lambda i, k: (i, k, 0))],
        out_specs=pl.BlockSpec((None, block_rows // 2, cols), lambda i, k: (i, k, 0)),
        out_shape=jax.ShapeDtypeStruct((n, rows // 2, cols), jnp.uint32),
        compiler_params=pltpu.CompilerParams(dimension_semantics=("arbitrary", "arbitrary")),
        name=name,
    )(w)


def _block_diag(w, group):
    depth, heads, d, _ = w.shape
    w = w.reshape(depth, heads // group, group, d, d)
    eye = jnp.eye(group, dtype=w.dtype)
    return jnp.einsum("lngij,gh->lngihj", w, eye).reshape(depth, heads // group, group * d, group * d)


def kernel(x, meta, norm_g, w_in, conv_a_w, conv_a_b, lru_wr, lru_br, lru_wi, lru_bi, lru_lambda,
           conv_b_w, w_out, final_g):
    depth = norm_g.shape[0]
    chans = lru_lambda.shape[-1]
    head_dim = lru_wr.shape[-1]
    assert w_in.shape[-1] == N_SPLITS * chans
    group = CHANNEL_BLOCK // head_dim
    gates = _block_diag(jnp.concatenate([lru_wr, lru_wi]), group)
    gates = _pack_bf16_rows(gates.reshape((-1,) + gates.shape[2:]), CHANNEL_BLOCK, "pack_gates")
    gates = gates.reshape(2, depth, -1, CHANNEL_BLOCK // 2, CHANNEL_BLOCK)
    stacks = (norm_g[:, None], _pack_bf16_rows(w_in, PACK_BLOCK_ROWS, "pack_w_in"), conv_a_w,
              conv_a_b[:, None], gates[0], lru_br[:, None], gates[1], lru_bi[:, None],
              lru_lambda[:, None], conv_b_w, _pack_bf16_rows(w_out, 2 * PACK_BLOCK_ROWS, "pack_w_out"))
    fg = final_g[None]
    zero_state = (jnp.zeros((TAIL_ROWS, chans), jnp.float32),
                  jnp.zeros((TAIL_ROWS, chans), jnp.float32),
                  jnp.zeros((1, chans), jnp.float32))
    h_meta = meta[None]
    h = x
    for layer in range(depth):
        last = layer == depth - 1
        h_meta, *state = _layer_call(h_meta, stacks, fg, zero_state, layer, rows=meta.shape[0],
                                     final_norm=False, emit_state=True, name=f"meta_layer{layer}")
        (h,) = _layer_call(h, stacks, fg, state, layer, rows=MAIN_TILE_ROWS, final_norm=last,
                           emit_state=False, name=f"main_layer{layer}")
    return h
```

```python
import functools

import jax
import jax.numpy as jnp
from jax import lax
from jax.experimental import pallas as pl
from jax.experimental.pallas import tpu as pltpu

LRU_C = 8.0
RMS_EPS = 1e-6
N_SPLITS = 6
NEG_LOG2_E = -1.4426950408889634

SUBLANES = 8
BF16_TILE_ROWS = 16
CHANNEL_BLOCK = 256
TAIL_ROWS = SUBLANES
MAIN_TILE_ROWS = 512
PACK_BLOCK_ROWS = 256
VMEM_LIMIT_BYTES = 63 * 1024 * 1024


def _sigmoid(x):
    return 1.0 / (1.0 + jnp.exp2(x * NEG_LOG2_E))


def _softplus(x):
    return jnp.maximum(x, 0.0) + jnp.log1p(jnp.exp(-jnp.abs(x)))


def _one_minus_square_of_exp(log_a, a):
    y = 2.0 * log_a
    u = a * a
    near_zero = jnp.where(u == 1.0, -y, (1.0 - u) * y / jnp.log(u))
    return jnp.where(y < -0.5, 1.0 - u, near_zero)


def _pack_rows(w):
    return pltpu.bitcast(w.astype(jnp.bfloat16), jnp.uint32)


def _as_bf16(packed):
    return pltpu.bitcast(packed, jnp.bfloat16)


def _rmsnorm(x, g):
    return x * lax.rsqrt(jnp.mean(x * x, axis=-1, keepdims=True) + RMS_EPS) * g


def _scan_rows(a, b, carry):
    rows, width = a.shape
    blocks = rows // SUBLANES
    a3 = a.reshape(blocks, SUBLANES, width)
    b3 = b.reshape(blocks, SUBLANES, width)
    row = lax.broadcasted_iota(jnp.int32, (blocks, SUBLANES, width), 1)
    shift = 1
    while shift < SUBLANES:
        keep = row >= shift
        a_prev = pltpu.roll(a3, shift, axis=1)
        b_prev = pltpu.roll(b3, shift, axis=1)
        b3 = jnp.where(keep, a3 * b_prev + b3, b3)
        a3 = jnp.where(keep, a3 * a_prev, a3)
        shift *= 2
    last = slice(SUBLANES - 1, SUBLANES)
    carry = jnp.broadcast_to(carry, (SUBLANES, width))
    out = []
    for k in range(blocks):
        out.append(a3[k] * carry + b3[k])
        carry = (jnp.broadcast_to(a3[k][last], (SUBLANES, width)) * carry
                 + jnp.broadcast_to(b3[k][last], (SUBLANES, width)))
    return jnp.concatenate(out, axis=0), carry[last]


def _layer_kernel(x_ref, meta_ref, g_ref, win_ref, caw_ref, cab_ref, wr_ref, br_ref, wi_ref, bi_ref,
                  lam_ref, cbw_ref, wout_ref, fg_ref, *rest, final_norm, emit_meta, pack_next):
    rest = list(rest)
    next_win_ref, next_wout_ref = (rest.pop(0), rest.pop(0)) if pack_next else (None, None)
    out_ref = rest.pop(0)
    meta_out_ref = rest.pop(0) if emit_meta else None
    next_win_out, next_wout_out = (rest.pop(0), rest.pop(0)) if pack_next else (None, None)
    xa_tail, cb_tail, h_carry, xa_tail0, cb_tail0, h_carry0, y_buf, y_meta = rest
    chans = lam_ref.shape[-1]
    f32, bf16 = jnp.float32, jnp.bfloat16

    def mix(x, y_ref):
        rows = x.shape[0]
        blocks = rows // SUBLANES
        hn = _rmsnorm(x, g_ref[...]).astype(bf16)

        def proj(split, c):
            col = split * chans + c * CHANNEL_BLOCK
            return jnp.dot(hn, _as_bf16(win_ref[:, col:col + CHANNEL_BLOCK]), preferred_element_type=f32)

        def as_blocks(v):
            return v.reshape(blocks, SUBLANES, v.shape[-1])

        def delayed(v3, tail, back):
            rolled = pltpu.roll(jnp.concatenate([tail[None], v3], axis=0), back, axis=1)
            row = lax.broadcasted_iota(jnp.int32, v3.shape, 1)
            return jnp.where(row < back, rolled[:-1], rolled[1:])

        def causal_conv(v3, tail, taps_ref, cs):
            n_taps = taps_ref.shape[0]
            acc = taps_ref[n_taps - 1:n_taps, cs] * v3
            for k in range(n_taps - 1):
                acc = acc + taps_ref[k:k + 1, cs] * delayed(v3, tail, n_taps - 1 - k)
            return acc

        for c in range(chans // CHANNEL_BLOCK):
            cs = slice(c * CHANNEL_BLOCK, (c + 1) * CHANNEL_BLOCK)
            ys = slice(chans + c * CHANNEL_BLOCK, chans + (c + 1) * CHANNEL_BLOCK)

            xa = as_blocks(proj(0, c))
            xc = causal_conv(xa, xa_tail[:, cs], caw_ref, cs) + cab_ref[:, cs]
            xa_tail[:, cs] = xa[blocks - 1]
            xcb = xc.reshape(rows, CHANNEL_BLOCK).astype(bf16)
            r = _sigmoid(jnp.dot(xcb, _as_bf16(wr_ref[c]), preferred_element_type=f32) + br_ref[:, cs])
            i = _sigmoid(jnp.dot(xcb, _as_bf16(wi_ref[c]), preferred_element_type=f32) + bi_ref[:, cs])
            log_a = (-LRU_C) * r * _softplus(-lam_ref[:, cs])
            a = jnp.exp(log_a)
            b = jnp.sqrt(_one_minus_square_of_exp(log_a, a)) * (i * xc.reshape(rows, CHANNEL_BLOCK))
            h, carry = _scan_rows(a, b, h_carry[:, cs])
            h_carry[:, cs] = carry
            ga = proj(1, c)
            y_ref[:, cs] = (h * (ga * _sigmoid(ga))).astype(bf16)

            gate_b = proj(2, c)
            cb = as_blocks(proj(3, c) * proj(4, c))
            cv = causal_conv(cb, cb_tail[:, cs], cbw_ref, cs).reshape(rows, CHANNEL_BLOCK)
            cb_tail[:, cs] = cb[blocks - 1]
            gb = proj(5, c)
            y_ref[:, ys] = (gate_b * cv * (gb * _sigmoid(gb))).astype(bf16)

    def project_out(x, y_ref):
        return x + jnp.dot(y_ref[...], _as_bf16(wout_ref[...]), preferred_element_type=f32)

    @pl.when((pl.program_id(0) == 0) & (pl.program_id(1) == 0))
    def _():
        xa_tail[...] = jnp.zeros_like(xa_tail)
        cb_tail[...] = jnp.zeros_like(cb_tail)
        h_carry[...] = jnp.zeros_like(h_carry)
        meta = meta_ref[...]
        mix(meta, y_meta)
        if emit_meta:
            meta_out_ref[...] = project_out(meta, y_meta)
        xa_tail0[...] = xa_tail[...]
        cb_tail0[...] = cb_tail[...]
        h_carry0[...] = h_carry[...]

    @pl.when(pl.program_id(1) == 0)
    def _():
        xa_tail[...] = xa_tail0[...]
        cb_tail[...] = cb_tail0[...]
        h_carry[...] = h_carry0[...]

    x = x_ref[...]
    mix(x, y_buf)
    out = project_out(x, y_buf)
    if final_norm:
        out = _rmsnorm(out, fg_ref[...])
    out_ref[...] = out

    if pack_next:
        next_win_out[...] = _pack_rows(next_win_ref[...])
        next_wout_out[...] = _pack_rows(next_wout_ref[...])


def _resident(shape, lead=()):
    block = (None,) * len(lead) + tuple(shape[len(lead):])
    index = tuple(lead) + (0,) * (len(shape) - len(lead))
    return pl.BlockSpec(block, lambda b, t: index, pipeline_mode=pl.Buffered(1))


def _layer_call(x, meta, p, layer, *, final_norm, emit_meta, pack_next, name):
    batch, length, d_model = x.shape
    meta_rows = meta.shape[0]
    chans = p["lam"].shape[-1]
    rows = MAIN_TILE_ROWS
    n_tiles = length // rows
    steps = batch * n_tiles
    assert length % rows == 0 and chans % CHANNEL_BLOCK == 0
    assert rows % BF16_TILE_ROWS == 0 and meta_rows % BF16_TILE_ROWS == 0
    f32 = jnp.float32
    tile = pl.BlockSpec((None, rows, d_model), lambda b, t: (b, t, 0))

    args = [x, meta, p["g"], p["win"][layer], p["caw"], p["cab"], p["gates"], p["br"], p["gates"],
            p["bi"], p["lam"], p["cbw"], p["wout"][layer], p["fg"]]
    in_specs = [tile, _resident(meta.shape), _resident(p["g"].shape, (layer,)),
                _resident(p["win"][layer].shape), _resident(p["caw"].shape, (layer,)),
                _resident(p["cab"].shape, (layer,)), _resident(p["gates"].shape, (0, layer)),
                _resident(p["br"].shape, (layer,)), _resident(p["gates"].shape, (1, layer)),
                _resident(p["bi"].shape, (layer,)), _resident(p["lam"].shape, (layer,)),
                _resident(p["cbw"].shape, (layer,)), _resident(p["wout"][layer].shape),
                _resident(p["fg"].shape)]
    out_shape = [jax.ShapeDtypeStruct(x.shape, f32)]
    out_specs = [tile]
    if emit_meta:
        out_shape.append(jax.ShapeDtypeStruct(meta.shape, f32))
        out_specs.append(pl.BlockSpec(meta.shape, lambda b, t: (0, 0)))
    if pack_next:
        for w in (p["w_in"], p["w_out"]):
            w_rows, w_cols = w.shape[1:]
            chunk = w_rows // steps
            assert w_rows % steps == 0 and chunk % BF16_TILE_ROWS == 0
            args.append(w)
            in_specs.append(pl.BlockSpec((None, chunk, w_cols),
                                         lambda b, t: (layer + 1, b * n_tiles + t, 0)))
            out_shape.append(jax.ShapeDtypeStruct((w_rows // 2, w_cols), jnp.uint32))
            out_specs.append(pl.BlockSpec((chunk // 2, w_cols), lambda b, t: (b * n_tiles + t, 0)))
    scratch = (
        [pltpu.VMEM((TAIL_ROWS, chans), f32),
         pltpu.VMEM((TAIL_ROWS, chans), f32),
         pltpu.VMEM((1, chans), f32)] * 2
        + [pltpu.VMEM((rows, 2 * chans), jnp.bfloat16),
           pltpu.VMEM((meta_rows, 2 * chans), jnp.bfloat16)])
    kern = functools.partial(_layer_kernel, final_norm=final_norm, emit_meta=emit_meta,
                             pack_next=pack_next)
    return pl.pallas_call(
        kern,
        grid=(batch, n_tiles),
        in_specs=in_specs,
        out_specs=out_specs,
        out_shape=out_shape,
        scratch_shapes=scratch,
        compiler_params=pltpu.CompilerParams(
            dimension_semantics=("arbitrary", "arbitrary"),
            vmem_limit_bytes=VMEM_LIMIT_BYTES),
        name=name,
    )(*args)


def _pack_kernel(w_ref, o_ref):
    o_ref[...] = _pack_rows(w_ref[...])


def _pack_layer(w, layer, block_rows, name):
    _, rows, cols = w.shape
    assert rows % block_rows == 0 and block_rows % BF16_TILE_ROWS == 0
    return pl.pallas_call(
        _pack_kernel,
        grid=(rows // block_rows,),
        in_specs=[pl.BlockSpec((None, block_rows, cols), lambda k: (layer, k, 0))],
        out_specs=pl.BlockSpec((block_rows // 2, cols), lambda k: (k, 0)),
        out_shape=jax.ShapeDtypeStruct((rows // 2, cols), jnp.uint32),
        compiler_params=pltpu.CompilerParams(dimension_semantics=("arbitrary",)),
        name=name,
    )(w)


def _pack_gates_kernel(wr_ref, wi_ref, o_ref):
    depth, heads, d, _ = wr_ref.shape
    group = CHANNEL_BLOCK // d
    zero = jnp.zeros((d, d), jnp.float32)
    for which, ref in enumerate((wr_ref, wi_ref)):
        for layer in range(depth):
            for n in range(heads // group):
                tile = jnp.concatenate(
                    [jnp.concatenate([ref[layer, n * group + g] if h == g else zero
                                      for h in range(group)], axis=1) for g in range(group)], axis=0)
                o_ref[which, layer, n] = _pack_rows(tile)


def _pack_gates(lru_wr, lru_wi):
    depth, heads, d, _ = lru_wr.shape
    assert CHANNEL_BLOCK % d == 0 and heads % (CHANNEL_BLOCK // d) == 0
    tiles = heads * d // CHANNEL_BLOCK
    return pl.pallas_call(
        _pack_gates_kernel,
        out_shape=jax.ShapeDtypeStruct((2, depth, tiles, CHANNEL_BLOCK // 2, CHANNEL_BLOCK), jnp.uint32),
        name="pack_gates",
    )(lru_wr, lru_wi)


def kernel(x, meta, norm_g, w_in, conv_a_w, conv_a_b, lru_wr, lru_br, lru_wi, lru_bi, lru_lambda,
           conv_b_w, w_out, final_g):
    depth = norm_g.shape[0]
    assert w_in.shape[-1] == N_SPLITS * lru_lambda.shape[-1]
    p = dict(g=norm_g[:, None], caw=conv_a_w, cab=conv_a_b[:, None], gates=_pack_gates(lru_wr, lru_wi),
             br=lru_br[:, None], bi=lru_bi[:, None], lam=lru_lambda[:, None], cbw=conv_b_w,
             fg=final_g[None], w_in=w_in, w_out=w_out,
             win=[_pack_layer(w_in, 0, PACK_BLOCK_ROWS, "pack_w_in")],
             wout=[_pack_layer(w_out, 0, 2 * PACK_BLOCK_ROWS, "pack_w_out")])
    h = x
    for layer in range(depth):
        last = layer == depth - 1
        h, *more = _layer_call(h, meta, p, layer, final_norm=last, emit_meta=not last,
                               pack_next=not last, name=f"layer{layer}")
        if not last:
            meta, next_win, next_wout = more
            p["win"].append(next_win)
            p["wout"].append(next_wout)
    return h
```

```python
import functools

import jax
import jax.numpy as jnp
from jax import lax
from jax.experimental import pallas as pl
from jax.experimental.pallas import tpu as pltpu

LRU_C = 8.0
RMS_EPS = 1e-6
N_SPLITS = 6
NEG_LOG2_E = -1.4426950408889634

SUBLANES = 8
BF16_TILE_ROWS = 16
CHANNEL_BLOCK = 256
TAIL_ROWS = SUBLANES
MAIN_TILE_ROWS = 512
PACK_BLOCK_ROWS = 256
VMEM_LIMIT_BYTES = 63 * 1024 * 1024


def _sigmoid(x):
    return 1.0 / (1.0 + jnp.exp2(x * NEG_LOG2_E))


def _softplus(x):
    return jnp.maximum(x, 0.0) + jnp.log1p(jnp.exp(-jnp.abs(x)))


def _one_minus_square_of_exp(log_a, a):
    y = 2.0 * log_a
    u = a * a
    near_zero = jnp.where(u == 1.0, -y, (1.0 - u) * y / jnp.log(u))
    return jnp.where(y < -0.5, 1.0 - u, near_zero)


def _pack_rows(w):
    return pltpu.bitcast(w.astype(jnp.bfloat16), jnp.uint32)


def _as_bf16(packed):
    return pltpu.bitcast(packed, jnp.bfloat16)


def _rmsnorm(x, g):
    return x * lax.rsqrt(jnp.mean(x * x, axis=-1, keepdims=True) + RMS_EPS) * g


def _scan_rows(a, b, carry):
    rows, width = a.shape
    blocks = rows // SUBLANES
    a3 = a.reshape(blocks, SUBLANES, width)
    b3 = b.reshape(blocks, SUBLANES, width)
    row = lax.broadcasted_iota(jnp.int32, (blocks, SUBLANES, width), 1)
    shift = 1
    while shift < SUBLANES:
        keep = row >= shift
        a_prev = pltpu.roll(a3, shift, axis=1)
        b_prev = pltpu.roll(b3, shift, axis=1)
        b3 = jnp.where(keep, a3 * b_prev + b3, b3)
        a3 = jnp.where(keep, a3 * a_prev, a3)
        shift *= 2
    last = slice(SUBLANES - 1, SUBLANES)
    carry = jnp.broadcast_to(carry, (SUBLANES, width))
    out = []
    for k in range(blocks):
        out.append(a3[k] * carry + b3[k])
        carry = (jnp.broadcast_to(a3[k][last], (SUBLANES, width)) * carry
                 + jnp.broadcast_to(b3[k][last], (SUBLANES, width)))
    return jnp.concatenate(out, axis=0), carry[last]


def _layer_kernel(x_ref, meta_ref, g_ref, win_ref, caw_ref, cab_ref, wr_ref, br_ref, wi_ref, bi_ref,
                  lam_ref, cbw_ref, wout_ref, fg_ref, *rest, layer, final_norm, emit_meta, pack_next):
    rest = list(rest)
    next_win_ref, next_wout_ref = (rest.pop(0), rest.pop(0)) if pack_next else (None, None)
    out_ref = rest.pop(0)
    meta_out_ref = rest.pop(0) if emit_meta else None
    next_win_out, next_wout_out = (rest.pop(0), rest.pop(0)) if pack_next else (None, None)
    xa_tail, cb_tail, h_carry, xa_tail0, cb_tail0, h_carry0, y_buf, y_meta = rest
    chans = lam_ref.shape[-1]
    f32, bf16 = jnp.float32, jnp.bfloat16
    this = slice(layer, layer + 1)

    def mix(x, y_ref):
        rows = x.shape[0]
        blocks = rows // SUBLANES
        hn = _rmsnorm(x, g_ref[this, :]).astype(bf16)

        def proj(split, c):
            col = split * chans + c * CHANNEL_BLOCK
            return jnp.dot(hn, _as_bf16(win_ref[:, col:col + CHANNEL_BLOCK]), preferred_element_type=f32)

        def as_blocks(v):
            return v.reshape(blocks, SUBLANES, v.shape[-1])

        def delayed(v3, tail, back):
            rolled = pltpu.roll(jnp.concatenate([tail[None], v3], axis=0), back, axis=1)
            row = lax.broadcasted_iota(jnp.int32, v3.shape, 1)
            return jnp.where(row < back, rolled[:-1], rolled[1:])

        def causal_conv(v3, tail, taps_ref, cs):
            n_taps = taps_ref.shape[0]
            acc = taps_ref[n_taps - 1:n_taps, cs] * v3
            for k in range(n_taps - 1):
                acc = acc + taps_ref[k:k + 1, cs] * delayed(v3, tail, n_taps - 1 - k)
            return acc

        for c in range(chans // CHANNEL_BLOCK):
            cs = slice(c * CHANNEL_BLOCK, (c + 1) * CHANNEL_BLOCK)
            ys = slice(chans + c * CHANNEL_BLOCK, chans + (c + 1) * CHANNEL_BLOCK)

            xa = as_blocks(proj(0, c))
            xc = causal_conv(xa, xa_tail[:, cs], caw_ref, cs) + cab_ref[this, cs]
            xa_tail[:, cs] = xa[blocks - 1]
            xcb = xc.reshape(rows, CHANNEL_BLOCK).astype(bf16)
            r = _sigmoid(jnp.dot(xcb, _as_bf16(wr_ref[c]), preferred_element_type=f32) + br_ref[this, cs])
            i = _sigmoid(jnp.dot(xcb, _as_bf16(wi_ref[c]), preferred_element_type=f32) + bi_ref[this, cs])
            log_a = (-LRU_C) * r * _softplus(-lam_ref[this, cs])
            a = jnp.exp(log_a)
            b = jnp.sqrt(_one_minus_square_of_exp(log_a, a)) * (i * xc.reshape(rows, CHANNEL_BLOCK))
            h, carry = _scan_rows(a, b, h_carry[:, cs])
            h_carry[:, cs] = carry
            ga = proj(1, c)
            y_ref[:, cs] = (h * (ga * _sigmoid(ga))).astype(bf16)

            gate_b = proj(2, c)
            cb = as_blocks(proj(3, c) * proj(4, c))
            cv = causal_conv(cb, cb_tail[:, cs], cbw_ref, cs).reshape(rows, CHANNEL_BLOCK)
            cb_tail[:, cs] = cb[blocks - 1]
            gb = proj(5, c)
            y_ref[:, ys] = (gate_b * cv * (gb * _sigmoid(gb))).astype(bf16)

    def project_out(x, y_ref):
        return x + jnp.dot(y_ref[...], _as_bf16(wout_ref[...]), preferred_element_type=f32)

    @pl.when((pl.program_id(0) == 0) & (pl.program_id(1) == 0))
    def _():
        xa_tail[...] = jnp.zeros_like(xa_tail)
        cb_tail[...] = jnp.zeros_like(cb_tail)
        h_carry[...] = jnp.zeros_like(h_carry)
        meta = meta_ref[...]
        mix(meta, y_meta)
        if emit_meta:
            meta_out_ref[...] = project_out(meta, y_meta)
        xa_tail0[...] = xa_tail[...]
        cb_tail0[...] = cb_tail[...]
        h_carry0[...] = h_carry[...]

    @pl.when(pl.program_id(1) == 0)
    def _():
        xa_tail[...] = xa_tail0[...]
        cb_tail[...] = cb_tail0[...]
        h_carry[...] = h_carry0[...]

    x = x_ref[...]
    mix(x, y_buf)
    out = project_out(x, y_buf)
    if final_norm:
        out = _rmsnorm(out, fg_ref[...])
    out_ref[...] = out

    if pack_next:
        next_win_out[...] = _pack_rows(next_win_ref[...])
        next_wout_out[...] = _pack_rows(next_wout_ref[...])


def _resident(shape, lead=()):
    block = (None,) * len(lead) + tuple(shape[len(lead):])
    index = tuple(lead) + (0,) * (len(shape) - len(lead))
    return pl.BlockSpec(block, lambda b, t: index, pipeline_mode=pl.Buffered(1))


def _layer_call(x, meta, p, layer, *, final_norm, emit_meta, pack_next, name):
    batch, length, d_model = x.shape
    meta_rows = meta.shape[0]
    chans = p["lam"].shape[-1]
    rows = MAIN_TILE_ROWS
    n_tiles = length // rows
    steps = batch * n_tiles
    assert length % rows == 0 and chans % CHANNEL_BLOCK == 0
    assert rows % BF16_TILE_ROWS == 0 and meta_rows % BF16_TILE_ROWS == 0
    f32 = jnp.float32
    tile = pl.BlockSpec((None, rows, d_model), lambda b, t: (b, t, 0))

    args = [x, meta, p["g"], p["win"][layer], p["caw"], p["cab"], p["gates"], p["br"], p["gates"],
            p["bi"], p["lam"], p["cbw"], p["wout"][layer], p["fg"]]
    in_specs = [tile, _resident(meta.shape), _resident(p["g"].shape),
                _resident(p["win"][layer].shape), _resident(p["caw"].shape, (layer,)),
                _resident(p["cab"].shape), _resident(p["gates"].shape, (0, layer)),
                _resident(p["br"].shape), _resident(p["gates"].shape, (1, layer)),
                _resident(p["bi"].shape), _resident(p["lam"].shape),
                _resident(p["cbw"].shape, (layer,)), _resident(p["wout"][layer].shape),
                _resident(p["fg"].shape)]
    out_shape = [jax.ShapeDtypeStruct(x.shape, f32)]
    out_specs = [tile]
    if emit_meta:
        out_shape.append(jax.ShapeDtypeStruct(meta.shape, f32))
        out_specs.append(pl.BlockSpec(meta.shape, lambda b, t: (0, 0)))
    if pack_next:
        for w in (p["w_in"], p["w_out"]):
            w_rows, w_cols = w.shape[1:]
            chunk = w_rows // steps
            assert w_rows % steps == 0 and chunk % BF16_TILE_ROWS == 0
            args.append(w)
            in_specs.append(pl.BlockSpec((None, chunk, w_cols),
                                         lambda b, t: (layer + 1, b * n_tiles + t, 0)))
            out_shape.append(jax.ShapeDtypeStruct((w_rows // 2, w_cols), jnp.uint32))
            out_specs.append(pl.BlockSpec((chunk // 2, w_cols), lambda b, t: (b * n_tiles + t, 0)))
    scratch = (
        [pltpu.VMEM((TAIL_ROWS, chans), f32),
         pltpu.VMEM((TAIL_ROWS, chans), f32),
         pltpu.VMEM((1, chans), f32)] * 2
        + [pltpu.VMEM((rows, 2 * chans), jnp.bfloat16),
           pltpu.VMEM((meta_rows, 2 * chans), jnp.bfloat16)])
    kern = functools.partial(_layer_kernel, layer=layer, final_norm=final_norm, emit_meta=emit_meta,
                             pack_next=pack_next)
    return pl.pallas_call(
        kern,
        grid=(batch, n_tiles),
        in_specs=in_specs,
        out_specs=out_specs,
        out_shape=out_shape,
        scratch_shapes=scratch,
        compiler_params=pltpu.CompilerParams(
            dimension_semantics=("arbitrary", "arbitrary"),
            vmem_limit_bytes=VMEM_LIMIT_BYTES),
        name=name,
    )(*args)


def _pack_kernel(w_ref, o_ref):
    o_ref[...] = _pack_rows(w_ref[...])


def _pack_layer(w, layer, block_rows, name):
    _, rows, cols = w.shape
    assert rows % block_rows == 0 and block_rows % BF16_TILE_ROWS == 0
    return pl.pallas_call(
        _pack_kernel,
        grid=(rows // block_rows,),
        in_specs=[pl.BlockSpec((None, block_rows, cols), lambda k: (layer, k, 0))],
        out_specs=pl.BlockSpec((block_rows // 2, cols), lambda k: (k, 0)),
        out_shape=jax.ShapeDtypeStruct((rows // 2, cols), jnp.uint32),
        compiler_params=pltpu.CompilerParams(dimension_semantics=("arbitrary",)),
        name=name,
    )(w)


def _pack_gates_kernel(wr_ref, wi_ref, o_ref):
    depth, heads, d, _ = wr_ref.shape
    group = CHANNEL_BLOCK // d
    zero = jnp.zeros((d, d), jnp.float32)
    for which, ref in enumerate((wr_ref, wi_ref)):
        for layer in range(depth):
            for n in range(heads // group):
                tile = jnp.concatenate(
                    [jnp.concatenate([ref[layer, n * group + g] if h == g else zero
                                      for h in range(group)], axis=1) for g in range(group)], axis=0)
                o_ref[which, layer, n] = _pack_rows(tile)


def _pack_gates(lru_wr, lru_wi):
    depth, heads, d, _ = lru_wr.shape
    assert CHANNEL_BLOCK % d == 0 and heads % (CHANNEL_BLOCK // d) == 0
    tiles = heads * d // CHANNEL_BLOCK
    return pl.pallas_call(
        _pack_gates_kernel,
        out_shape=jax.ShapeDtypeStruct((2, depth, tiles, CHANNEL_BLOCK // 2, CHANNEL_BLOCK), jnp.uint32),
        name="pack_gates",
    )(lru_wr, lru_wi)


def kernel(x, meta, norm_g, w_in, conv_a_w, conv_a_b, lru_wr, lru_br, lru_wi, lru_bi, lru_lambda,
           conv_b_w, w_out, final_g):
    depth = norm_g.shape[0]
    assert w_in.shape[-1] == N_SPLITS * lru_lambda.shape[-1]
    p = dict(g=norm_g, caw=conv_a_w, cab=conv_a_b, gates=_pack_gates(lru_wr, lru_wi),
             br=lru_br, bi=lru_bi, lam=lru_lambda, cbw=conv_b_w,
             fg=final_g[None], w_in=w_in, w_out=w_out,
             win=[_pack_layer(w_in, 0, PACK_BLOCK_ROWS, "pack_w_in")],
             wout=[_pack_layer(w_out, 0, 2 * PACK_BLOCK_ROWS, "pack_w_out")])
    h = x
    for layer in range(depth):
        last = layer == depth - 1
        h, *more = _layer_call(h, meta, p, layer, final_norm=last, emit_meta=not last,
                               pack_next=not last, name=f"layer{layer}")
        if not last:
            meta, next_win, next_wout = more
            p["win"].append(next_win)
            p["wout"].append(next_wout)
    return h
```

```python
import functools

import jax
import jax.numpy as jnp
from jax import lax
from jax.experimental import pallas as pl
from jax.experimental.pallas import tpu as pltpu

LRU_C = 8.0
RMS_EPS = 1e-6
N_SPLITS = 6
NEG_LOG2_E = -1.4426950408889634

SUBLANES = 8
BF16_TILE_ROWS = 16
CHANNEL_BLOCK = 256
TAIL_ROWS = SUBLANES
MAIN_TILE_ROWS = 512
VMEM_LIMIT_BYTES = 63 * 1024 * 1024


def _sigmoid(x):
    return 1.0 / (1.0 + jnp.exp2(x * NEG_LOG2_E))


def _softplus(x):
    return jnp.maximum(x, 0.0) + jnp.log1p(jnp.exp(-jnp.abs(x)))


def _one_minus_square_of_exp(log_a, a):
    y = 2.0 * log_a
    u = a * a
    near_zero = jnp.where(u == 1.0, -y, (1.0 - u) * y / jnp.log(u))
    return jnp.where(y < -0.5, 1.0 - u, near_zero)


def _pack_rows(w):
    return pltpu.bitcast(w.astype(jnp.bfloat16), jnp.uint32)


def _as_bf16(packed):
    return pltpu.bitcast(packed, jnp.bfloat16)


def _rmsnorm(x, g):
    return x * lax.rsqrt(jnp.mean(x * x, axis=-1, keepdims=True) + RMS_EPS) * g


def _scan_rows(a, b, carry):
    rows, width = a.shape
    blocks = rows // SUBLANES
    a3 = a.reshape(blocks, SUBLANES, width)
    b3 = b.reshape(blocks, SUBLANES, width)
    row = lax.broadcasted_iota(jnp.int32, (blocks, SUBLANES, width), 1)
    shift = 1
    while shift < SUBLANES:
        keep = row >= shift
        a_prev = pltpu.roll(a3, shift, axis=1)
        b_prev = pltpu.roll(b3, shift, axis=1)
        b3 = jnp.where(keep, a3 * b_prev + b3, b3)
        a3 = jnp.where(keep, a3 * a_prev, a3)
        shift *= 2
    last = slice(SUBLANES - 1, SUBLANES)
    carry = jnp.broadcast_to(carry, (SUBLANES, width))
    out = []
    for k in range(blocks):
        out.append(a3[k] * carry + b3[k])
        carry = (jnp.broadcast_to(a3[k][last], (SUBLANES, width)) * carry
                 + jnp.broadcast_to(b3[k][last], (SUBLANES, width)))
    return jnp.concatenate(out, axis=0), carry[last]


def _layer_kernel(*refs, layer, n_tiles, own_steps, final_norm, emit_meta, pack_next):
    refs = list(refs)
    pack_own = own_steps > 0
    x_ref, meta_ref, g_ref = refs.pop(0), refs.pop(0), refs.pop(0)
    win_ref = None if pack_own else refs.pop(0)
    caw_ref, cab_ref, wr_ref, br_ref, wi_ref, bi_ref, lam_ref, cbw_ref = [refs.pop(0) for _ in range(8)]
    wout_ref = None if pack_own else refs.pop(0)
    fg_ref = refs.pop(0)
    win_chunk, wout_chunk = (refs.pop(0), refs.pop(0)) if pack_own or pack_next else (None, None)
    out_ref = refs.pop(0)
    meta_out_ref = refs.pop(0) if emit_meta else None
    next_win_out, next_wout_out = (refs.pop(0), refs.pop(0)) if pack_next else (None, None)
    xa_tail, cb_tail, h_carry, xa_tail0, cb_tail0, h_carry0, y_buf, y_meta = refs[:8]
    if pack_own:
        win_ref, wout_ref = refs[8:]
    chans = lam_ref.shape[-1]
    f32, bf16 = jnp.float32, jnp.bfloat16
    this = slice(layer, layer + 1)
    step = pl.program_id(0)
    tile_step = step - own_steps

    def mix(x, y_ref):
        rows = x.shape[0]
        blocks = rows // SUBLANES
        hn = _rmsnorm(x, g_ref[this, :]).astype(bf16)

        def proj(split, c):
            col = split * chans + c * CHANNEL_BLOCK
            return jnp.dot(hn, _as_bf16(win_ref[:, col:col + CHANNEL_BLOCK]), preferred_element_type=f32)

        def as_blocks(v):
            return v.reshape(blocks, SUBLANES, v.shape[-1])

        def delayed(v3, tail, back):
            rolled = pltpu.roll(jnp.concatenate([tail[None], v3], axis=0), back, axis=1)
            row = lax.broadcasted_iota(jnp.int32, v3.shape, 1)
            return jnp.where(row < back, rolled[:-1], rolled[1:])

        def causal_conv(v3, tail, taps_ref, cs):
            n_taps = taps_ref.shape[0]
            acc = taps_ref[n_taps - 1:n_taps, cs] * v3
            for k in range(n_taps - 1):
                acc = acc + taps_ref[k:k + 1, cs] * delayed(v3, tail, n_taps - 1 - k)
            return acc

        for c in range(chans // CHANNEL_BLOCK):
            cs = slice(c * CHANNEL_BLOCK, (c + 1) * CHANNEL_BLOCK)
            ys = slice(chans + c * CHANNEL_BLOCK, chans + (c + 1) * CHANNEL_BLOCK)

            xa = as_blocks(proj(0, c))
            xc = causal_conv(xa, xa_tail[:, cs], caw_ref, cs) + cab_ref[this, cs]
            xa_tail[:, cs] = xa[blocks - 1]
            xcb = xc.reshape(rows, CHANNEL_BLOCK).astype(bf16)
            r = _sigmoid(jnp.dot(xcb, _as_bf16(wr_ref[c]), preferred_element_type=f32) + br_ref[this, cs])
            i = _sigmoid(jnp.dot(xcb, _as_bf16(wi_ref[c]), preferred_element_type=f32) + bi_ref[this, cs])
            log_a = (-LRU_C) * r * _softplus(-lam_ref[this, cs])
            a = jnp.exp(log_a)
            b = jnp.sqrt(_one_minus_square_of_exp(log_a, a)) * (i * xc.reshape(rows, CHANNEL_BLOCK))
            h, carry = _scan_rows(a, b, h_carry[:, cs])
            h_carry[:, cs] = carry
            ga = proj(1, c)
            y_ref[:, cs] = (h * (ga * _sigmoid(ga))).astype(bf16)

            gate_b = proj(2, c)
            cb = as_blocks(proj(3, c) * proj(4, c))
            cv = causal_conv(cb, cb_tail[:, cs], cbw_ref, cs).reshape(rows, CHANNEL_BLOCK)
            cb_tail[:, cs] = cb[blocks - 1]
            gb = proj(5, c)
            y_ref[:, ys] = (gate_b * cv * (gb * _sigmoid(gb))).astype(bf16)

    def project_out(x, y_ref):
        return x + jnp.dot(y_ref[...], _as_bf16(wout_ref[...]), preferred_element_type=f32)

    if pack_own:
        @pl.when(step < own_steps)
        def _():
            for chunk, dst in ((win_chunk, win_ref), (wout_chunk, wout_ref)):
                rows = chunk.shape[0] // 2
                dst[pl.ds(pl.multiple_of(step * rows, rows), rows), :] = _pack_rows(chunk[...])

    @pl.when(tile_step == 0)
    def _():
        xa_tail[...] = jnp.zeros_like(xa_tail)
        cb_tail[...] = jnp.zeros_like(cb_tail)
        h_carry[...] = jnp.zeros_like(h_carry)
        meta = meta_ref[...]
        mix(meta, y_meta)
        if emit_meta:
            meta_out_ref[...] = project_out(meta, y_meta)
        xa_tail0[...] = xa_tail[...]
        cb_tail0[...] = cb_tail[...]
        h_carry0[...] = h_carry[...]

    @pl.when((tile_step >= 0) & (tile_step % n_tiles == 0))
    def _():
        xa_tail[...] = xa_tail0[...]
        cb_tail[...] = cb_tail0[...]
        h_carry[...] = h_carry0[...]

    def tile():
        x = x_ref[...]
        mix(x, y_buf)
        out = project_out(x, y_buf)
        if final_norm:
            out = _rmsnorm(out, fg_ref[...])
        out_ref[...] = out
        if pack_next:
            next_win_out[...] = _pack_rows(win_chunk[...])
            next_wout_out[...] = _pack_rows(wout_chunk[...])

    if pack_own:
        pl.when(tile_step >= 0)(tile)
    else:
        tile()


def _resident(shape, lead=()):
    block = (None,) * len(lead) + tuple(shape[len(lead):])
    index = tuple(lead) + (0,) * (len(shape) - len(lead))
    return pl.BlockSpec(block, lambda s: index, pipeline_mode=pl.Buffered(1))


def _layer_call(x, meta, p, layer, *, final_norm, emit_meta, pack_own, pack_next, name):
    batch, length, d_model = x.shape
    meta_rows = meta.shape[0]
    chans = p["lam"].shape[-1]
    rows = MAIN_TILE_ROWS
    n_tiles = length // rows
    tile_steps = batch * n_tiles
    own_steps = tile_steps if pack_own else 0
    assert length % rows == 0 and chans % CHANNEL_BLOCK == 0
    assert rows % BF16_TILE_ROWS == 0 and meta_rows % BF16_TILE_ROWS == 0
    f32 = jnp.float32

    def tile_index(s):
        t = jnp.maximum(s - own_steps, 0)
        return (t // n_tiles, t % n_tiles, 0)

    tile = pl.BlockSpec((None, rows, d_model), tile_index)
    args = [x, meta, p["g"]]
    in_specs = [tile, _resident(meta.shape), _resident(p["g"].shape)]
    if not pack_own:
        args.append(p["win"][layer])
        in_specs.append(_resident(p["win"][layer].shape))
    args += [p["caw"], p["cab"], p["gates"], p["br"], p["gates"], p["bi"], p["lam"], p["cbw"]]
    in_specs += [_resident(p["caw"].shape, (layer,)), _resident(p["cab"].shape),
                 _resident(p["gates"].shape, (0, layer)), _resident(p["br"].shape),
                 _resident(p["gates"].shape, (1, layer)), _resident(p["bi"].shape),
                 _resident(p["lam"].shape), _resident(p["cbw"].shape, (layer,))]
    if not pack_own:
        args.append(p["wout"][layer])
        in_specs.append(_resident(p["wout"][layer].shape))
    args.append(p["fg"])
    in_specs.append(_resident(p["fg"].shape))

    out_shape = [jax.ShapeDtypeStruct(x.shape, f32)]
    out_specs = [tile]
    if emit_meta:
        out_shape.append(jax.ShapeDtypeStruct(meta.shape, f32))
        out_specs.append(pl.BlockSpec(meta.shape, lambda s: (0, 0)))
    scratch = (
        [pltpu.VMEM((TAIL_ROWS, chans), f32),
         pltpu.VMEM((TAIL_ROWS, chans), f32),
         pltpu.VMEM((1, chans), f32)] * 2
        + [pltpu.VMEM((rows, 2 * chans), jnp.bfloat16),
           pltpu.VMEM((meta_rows, 2 * chans), jnp.bfloat16)])
    if pack_own or pack_next:
        for w in (p["w_in"], p["w_out"]):
            w_rows, w_cols = w.shape[1:]
            chunk = w_rows // tile_steps
            assert w_rows % tile_steps == 0 and chunk % BF16_TILE_ROWS == 0
            args.append(w)
            if pack_own and pack_next:
                chunk_index = lambda s: (layer + s // own_steps, s % own_steps, 0)
            elif pack_own:
                chunk_index = lambda s: (layer, jnp.minimum(s, own_steps - 1), 0)
            else:
                chunk_index = lambda s: (layer + 1, s, 0)
            in_specs.append(pl.BlockSpec((None, chunk, w_cols), chunk_index))
            if pack_next:
                out_shape.append(jax.ShapeDtypeStruct((w_rows // 2, w_cols), jnp.uint32))
                out_specs.append(pl.BlockSpec((chunk // 2, w_cols),
                                              lambda s: (jnp.maximum(s - own_steps, 0), 0)))
            if pack_own:
                scratch.append(pltpu.VMEM((w_rows // 2, w_cols), jnp.uint32))
    kern = functools.partial(_layer_kernel, layer=layer, n_tiles=n_tiles, own_steps=own_steps,
                             final_norm=final_norm, emit_meta=emit_meta, pack_next=pack_next)
    return pl.pallas_call(
        kern,
        grid=(own_steps + tile_steps,),
        in_specs=in_specs,
        out_specs=out_specs,
        out_shape=out_shape,
        scratch_shapes=scratch,
        compiler_params=pltpu.CompilerParams(
            dimension_semantics=("arbitrary",),
            vmem_limit_bytes=VMEM_LIMIT_BYTES),
        name=name,
    )(*args)


def _pack_gates_kernel(wr_ref, wi_ref, o_ref):
    depth, heads, d, _ = wr_ref.shape
    group = CHANNEL_BLOCK // d
    zero = jnp.zeros((d, d), jnp.float32)
    for which, ref in enumerate((wr_ref, wi_ref)):
        for layer in range(depth):
            for n in range(heads // group):
                tile = jnp.concatenate(
                    [jnp.concatenate([ref[layer, n * group + g] if h == g else zero
                                      for h in range(group)], axis=1) for g in range(group)], axis=0)
                o_ref[which, layer, n] = _pack_rows(tile)


def _pack_gates(lru_wr, lru_wi):
    depth, heads, d, _ = lru_wr.shape
    assert CHANNEL_BLOCK % d == 0 and heads % (CHANNEL_BLOCK // d) == 0
    tiles = heads * d // CHANNEL_BLOCK
    return pl.pallas_call(
        _pack_gates_kernel,
        out_shape=jax.ShapeDtypeStruct((2, depth, tiles, CHANNEL_BLOCK // 2, CHANNEL_BLOCK), jnp.uint32),
        name="pack_gates",
    )(lru_wr, lru_wi)


def kernel(x, meta, norm_g, w_in, conv_a_w, conv_a_b, lru_wr, lru_br, lru_wi, lru_bi, lru_lambda,
           conv_b_w, w_out, final_g):
    depth = norm_g.shape[0]
    assert w_in.shape[-1] == N_SPLITS * lru_lambda.shape[-1]
    p = dict(g=norm_g, caw=conv_a_w, cab=conv_a_b, gates=_pack_gates(lru_wr, lru_wi),
             br=lru_br, bi=lru_bi, lam=lru_lambda, cbw=conv_b_w, fg=final_g[None],
             w_in=w_in, w_out=w_out, win=[None], wout=[None])
    h = x
    for layer in range(depth):
        last = layer == depth - 1
        h, *more = _layer_call(h, meta, p, layer, final_norm=last, emit_meta=not last,
                               pack_own=layer == 0, pack_next=not last, name=f"layer{layer}")
        if not last:
            meta, next_win, next_wout = more
            p["win"].append(next_win)
            p["wout"].append(next_wout)
    return h
```

```python
import functools

import jax
import jax.numpy as jnp
from jax import lax
from jax.experimental import pallas as pl
from jax.experimental.pallas import tpu as pltpu

LRU_C = 8.0
RMS_EPS = 1e-6
N_SPLITS = 6
NEG_LOG2_E = -1.4426950408889634

SUBLANES = 8
BF16_TILE_ROWS = 16
CHANNEL_BLOCK = 256
TAIL_ROWS = SUBLANES
MAIN_TILE_ROWS = 512
VMEM_LIMIT_BYTES = 63 * 1024 * 1024


def _sigmoid(x):
    return 1.0 / (1.0 + jnp.exp2(x * NEG_LOG2_E))


def _softplus(x):
    return jnp.maximum(x, 0.0) + jnp.log1p(jnp.exp(-jnp.abs(x)))


def _one_minus_square_of_exp(log_a, a):
    y = 2.0 * log_a
    u = a * a
    near_zero = jnp.where(u == 1.0, -y, (1.0 - u) * y / jnp.log(u))
    return jnp.where(y < -0.5, 1.0 - u, near_zero)


def _pack_rows(w):
    return pltpu.bitcast(w.astype(jnp.bfloat16), jnp.uint32)


def _as_bf16(packed):
    return pltpu.bitcast(packed, jnp.bfloat16)


def _rmsnorm(x, g):
    return x * lax.rsqrt(jnp.mean(x * x, axis=-1, keepdims=True) + RMS_EPS) * g


def _scan_rows(a, b, carry):
    rows, width = a.shape
    blocks = rows // SUBLANES
    a3 = a.reshape(blocks, SUBLANES, width)
    b3 = b.reshape(blocks, SUBLANES, width)
    row = lax.broadcasted_iota(jnp.int32, (blocks, SUBLANES, width), 1)
    shift = 1
    while shift < SUBLANES:
        keep = row >= shift
        a_prev = pltpu.roll(a3, shift, axis=1)
        b_prev = pltpu.roll(b3, shift, axis=1)
        b3 = jnp.where(keep, a3 * b_prev + b3, b3)
        a3 = jnp.where(keep, a3 * a_prev, a3)
        shift *= 2
    last = slice(SUBLANES - 1, SUBLANES)
    carry = jnp.broadcast_to(carry, (SUBLANES, width))
    out = []
    for k in range(blocks):
        out.append(a3[k] * carry + b3[k])
        carry = (jnp.broadcast_to(a3[k][last], (SUBLANES, width)) * carry
                 + jnp.broadcast_to(b3[k][last], (SUBLANES, width)))
    return jnp.concatenate(out, axis=0), carry[last]


def _layer_kernel(*refs, layer, n_tiles, own_steps, final_norm, emit_meta, pack_next):
    refs = list(refs)
    pack_own = own_steps > 0
    x_ref, meta_ref, g_ref = refs.pop(0), refs.pop(0), refs.pop(0)
    win_ref = None if pack_own else refs.pop(0)
    caw_ref, cab_ref, wr_ref, br_ref, wi_ref, bi_ref, lam_ref, cbw_ref = [refs.pop(0) for _ in range(8)]
    wout_ref = None if pack_own else refs.pop(0)
    fg_ref = refs.pop(0)
    win_chunk, wout_chunk = (refs.pop(0), refs.pop(0)) if pack_own or pack_next else (None, None)
    out_ref = refs.pop(0)
    meta_out_ref = refs.pop(0) if emit_meta else None
    next_win_out, next_wout_out = (refs.pop(0), refs.pop(0)) if pack_next else (None, None)
    xa_tail, cb_tail, h_carry, xa_tail0, cb_tail0, h_carry0, y_buf, y_meta = refs[:8]
    if pack_own:
        win_ref, wout_ref = refs[8:]
    chans = lam_ref.shape[-1]
    f32, bf16 = jnp.float32, jnp.bfloat16
    this = slice(layer, layer + 1)
    step = pl.program_id(0)
    tile_step = step - own_steps

    def mix(x, y_ref):
        rows = x.shape[0]
        blocks = rows // SUBLANES
        hn = _rmsnorm(x, g_ref[this, :]).astype(bf16)

        def proj(split, c):
            col = split * chans + c * CHANNEL_BLOCK
            return jnp.dot(hn, _as_bf16(win_ref[:, col:col + CHANNEL_BLOCK]), preferred_element_type=f32)

        def as_blocks(v):
            return v.reshape(blocks, SUBLANES, v.shape[-1])

        def delayed(v3, tail, back):
            rolled = pltpu.roll(jnp.concatenate([tail[None], v3], axis=0), back, axis=1)
            row = lax.broadcasted_iota(jnp.int32, v3.shape, 1)
            return jnp.where(row < back, rolled[:-1], rolled[1:])

        def causal_conv(v3, tail, taps_ref, cs):
            n_taps = taps_ref.shape[0]
            acc = taps_ref[n_taps - 1:n_taps, cs] * v3
            for k in range(n_taps - 1):
                acc = acc + taps_ref[k:k + 1, cs] * delayed(v3, tail, n_taps - 1 - k)
            return acc

        for c in range(chans // CHANNEL_BLOCK):
            cs = slice(c * CHANNEL_BLOCK, (c + 1) * CHANNEL_BLOCK)
            ys = slice(chans + c * CHANNEL_BLOCK, chans + (c + 1) * CHANNEL_BLOCK)

            xa = as_blocks(proj(0, c))
            xc = causal_conv(xa, xa_tail[:, cs], caw_ref, cs) + cab_ref[this, cs]
            xa_tail[:, cs] = xa[blocks - 1]
            xcb = xc.reshape(rows, CHANNEL_BLOCK).astype(bf16)
            r_pre = jnp.dot(xcb, _as_bf16(wr_ref[c]), preferred_element_type=f32)
            i_pre = jnp.dot(xcb, _as_bf16(wi_ref[c]), preferred_element_type=f32)
            xc2 = xc.reshape(rows, CHANNEL_BLOCK)

            def recur(lo, hi, carry):
                r = _sigmoid(r_pre[lo:hi] + br_ref[this, cs])
                i = _sigmoid(i_pre[lo:hi] + bi_ref[this, cs])
                log_a = (-LRU_C) * r * _softplus(-lam_ref[this, cs])
                a = jnp.exp(log_a)
                b = jnp.sqrt(_one_minus_square_of_exp(log_a, a)) * (i * xc2[lo:hi])
                return _scan_rows(a, b, carry)

            half = rows // 2 if rows % (2 * SUBLANES) == 0 and rows > BF16_TILE_ROWS else rows
            h, carry = recur(0, half, h_carry[:, cs])
            gate_c = proj(3, c)
            if half < rows:
                h_hi, carry = recur(half, rows, carry)
                h = jnp.concatenate([h, h_hi], axis=0)
            h_carry[:, cs] = carry
            ga = proj(1, c)
            y_ref[:, cs] = (h * (ga * _sigmoid(ga))).astype(bf16)

            gate_b = proj(2, c)
            cb = as_blocks(gate_c * proj(4, c))
            cv = causal_conv(cb, cb_tail[:, cs], cbw_ref, cs).reshape(rows, CHANNEL_BLOCK)
            cb_tail[:, cs] = cb[blocks - 1]
            gb = proj(5, c)
            y_ref[:, ys] = (gate_b * cv * (gb * _sigmoid(gb))).astype(bf16)

    def project_out(x, y_ref):
        return x + jnp.dot(y_ref[...], _as_bf16(wout_ref[...]), preferred_element_type=f32)

    if pack_own:
        @pl.when(step < own_steps)
        def _():
            for chunk, dst in ((win_chunk, win_ref), (wout_chunk, wout_ref)):
                rows = chunk.shape[0] // 2
                dst[pl.ds(pl.multiple_of(step * rows, rows), rows), :] = _pack_rows(chunk[...])

    @pl.when(tile_step == 0)
    def _():
        xa_tail[...] = jnp.zeros_like(xa_tail)
        cb_tail[...] = jnp.zeros_like(cb_tail)
        h_carry[...] = jnp.zeros_like(h_carry)
        meta = meta_ref[...]
        mix(meta, y_meta)
        if emit_meta:
            meta_out_ref[...] = project_out(meta, y_meta)
        xa_tail0[...] = xa_tail[...]
        cb_tail0[...] = cb_tail[...]
        h_carry0[...] = h_carry[...]

    @pl.when((tile_step >= 0) & (tile_step % n_tiles == 0))
    def _():
        xa_tail[...] = xa_tail0[...]
        cb_tail[...] = cb_tail0[...]
        h_carry[...] = h_carry0[...]

    def tile():
        x = x_ref[...]
        mix(x, y_buf)
        out = project_out(x, y_buf)
        if final_norm:
            out = _rmsnorm(out, fg_ref[...])
        out_ref[...] = out
        if pack_next:
            next_win_out[...] = _pack_rows(win_chunk[...])
            next_wout_out[...] = _pack_rows(wout_chunk[...])

    if pack_own:
        pl.when(tile_step >= 0)(tile)
    else:
        tile()


def _resident(shape, lead=()):
    block = (None,) * len(lead) + tuple(shape[len(lead):])
    index = tuple(lead) + (0,) * (len(shape) - len(lead))
    return pl.BlockSpec(block, lambda s: index, pipeline_mode=pl.Buffered(1))


def _layer_call(x, meta, p, layer, *, final_norm, emit_meta, pack_own, pack_next, name):
    batch, length, d_model = x.shape
    meta_rows = meta.shape[0]
    chans = p["lam"].shape[-1]
    rows = MAIN_TILE_ROWS
    n_tiles = length // rows
    tile_steps = batch * n_tiles
    own_steps = tile_steps if pack_own else 0
    assert length % rows == 0 and chans % CHANNEL_BLOCK == 0
    assert rows % BF16_TILE_ROWS == 0 and meta_rows % BF16_TILE_ROWS == 0
    f32 = jnp.float32

    def tile_index(s):
        t = jnp.maximum(s - own_steps, 0)
        return (t // n_tiles, t % n_tiles, 0)

    tile = pl.BlockSpec((None, rows, d_model), tile_index)
    args = [x, meta, p["g"]]
    in_specs = [tile, _resident(meta.shape), _resident(p["g"].shape)]
    if not pack_own:
        args.append(p["win"][layer])
        in_specs.append(_resident(p["win"][layer].shape))
    args += [p["caw"], p["cab"], p["gates"], p["br"], p["gates"], p["bi"], p["lam"], p["cbw"]]
    in_specs += [_resident(p["caw"].shape, (layer,)), _resident(p["cab"].shape),
                 _resident(p["gates"].shape, (0, layer)), _resident(p["br"].shape),
                 _resident(p["gates"].shape, (1, layer)), _resident(p["bi"].shape),
                 _resident(p["lam"].shape), _resident(p["cbw"].shape, (layer,))]
    if not pack_own:
        args.append(p["wout"][layer])
        in_specs.append(_resident(p["wout"][layer].shape))
    args.append(p["fg"])
    in_specs.append(_resident(p["fg"].shape))

    out_shape = [jax.ShapeDtypeStruct(x.shape, f32)]
    out_specs = [tile]
    if emit_meta:
        out_shape.append(jax.ShapeDtypeStruct(meta.shape, f32))
        out_specs.append(pl.BlockSpec(meta.shape, lambda s: (0, 0)))
    scratch = (
        [pltpu.VMEM((TAIL_ROWS, chans), f32),
         pltpu.VMEM((TAIL_ROWS, chans), f32),
         pltpu.VMEM((1, chans), f32)] * 2
        + [pltpu.VMEM((rows, 2 * chans), jnp.bfloat16),
           pltpu.VMEM((meta_rows, 2 * chans), jnp.bfloat16)])
    if pack_own or pack_next:
        for w in (p["w_in"], p["w_out"]):
            w_rows, w_cols = w.shape[1:]
            chunk = w_rows // tile_steps
            assert w_rows % tile_steps == 0 and chunk % BF16_TILE_ROWS == 0
            args.append(w)
            if pack_own and pack_next:
                chunk_index = lambda s: (layer + s // own_steps, s % own_steps, 0)
            elif pack_own:
                chunk_index = lambda s: (layer, jnp.minimum(s, own_steps - 1), 0)
            else:
                chunk_index = lambda s: (layer + 1, s, 0)
            in_specs.append(pl.BlockSpec((None, chunk, w_cols), chunk_index))
            if pack_next:
                out_shape.append(jax.ShapeDtypeStruct((w_rows // 2, w_cols), jnp.uint32))
                out_specs.append(pl.BlockSpec((chunk // 2, w_cols),
                                              lambda s: (jnp.maximum(s - own_steps, 0), 0)))
            if pack_own:
                scratch.append(pltpu.VMEM((w_rows // 2, w_cols), jnp.uint32))
    kern = functools.partial(_layer_kernel, layer=layer, n_tiles=n_tiles, own_steps=own_steps,
                             final_norm=final_norm, emit_meta=emit_meta, pack_next=pack_next)
    return pl.pallas_call(
        kern,
        grid=(own_steps + tile_steps,),
        in_specs=in_specs,
        out_specs=out_specs,
        out_shape=out_shape,
        scratch_shapes=scratch,
        compiler_params=pltpu.CompilerParams(
            dimension_semantics=("arbitrary",),
            vmem_limit_bytes=VMEM_LIMIT_BYTES),
        name=name,
    )(*args)


def _pack_gates_kernel(wr_ref, wi_ref, o_ref):
    depth, heads, d, _ = wr_ref.shape
    group = CHANNEL_BLOCK // d
    zero = jnp.zeros((d, d), jnp.float32)
    for which, ref in enumerate((wr_ref, wi_ref)):
        for layer in range(depth):
            for n in range(heads // group):
                tile = jnp.concatenate(
                    [jnp.concatenate([ref[layer, n * group + g] if h == g else zero
                                      for h in range(group)], axis=1) for g in range(group)], axis=0)
                o_ref[which, layer, n] = _pack_rows(tile)


def _pack_gates(lru_wr, lru_wi):
    depth, heads, d, _ = lru_wr.shape
    assert CHANNEL_BLOCK % d == 0 and heads % (CHANNEL_BLOCK // d) == 0
    tiles = heads * d // CHANNEL_BLOCK
    return pl.pallas_call(
        _pack_gates_kernel,
        out_shape=jax.ShapeDtypeStruct((2, depth, tiles, CHANNEL_BLOCK // 2, CHANNEL_BLOCK), jnp.uint32),
        name="pack_gates",
    )(lru_wr, lru_wi)


def kernel(x, meta, norm_g, w_in, conv_a_w, conv_a_b, lru_wr, lru_br, lru_wi, lru_bi, lru_lambda,
           conv_b_w, w_out, final_g):
    depth = norm_g.shape[0]
    assert w_in.shape[-1] == N_SPLITS * lru_lambda.shape[-1]
    p = dict(g=norm_g, caw=conv_a_w, cab=conv_a_b, gates=_pack_gates(lru_wr, lru_wi),
             br=lru_br, bi=lru_bi, lam=lru_lambda, cbw=conv_b_w, fg=final_g[None],
             w_in=w_in, w_out=w_out, win=[None], wout=[None])
    h = x
    for layer in range(depth):
        last = layer == depth - 1
        h, *more = _layer_call(h, meta, p, layer, final_norm=last, emit_meta=not last,
                               pack_own=layer == 0, pack_next=not last, name=f"layer{layer}")
        if not last:
            meta, next_win, next_wout = more
            p["win"].append(next_win)
            p["wout"].append(next_wout)
    return h
```

```python
import functools

import jax
import jax.numpy as jnp
from jax import lax
from jax.experimental import pallas as pl
from jax.experimental.pallas import tpu as pltpu

LRU_C = 8.0
RMS_EPS = 1e-6
N_SPLITS = 6
NEG_LOG2_E = -1.4426950408889634

SUBLANES = 8
BF16_TILE_ROWS = 16
CHANNEL_BLOCK = 256
TAIL_ROWS = SUBLANES
MAIN_TILE_ROWS = 512
VMEM_LIMIT_BYTES = 63 * 1024 * 1024


def _sigmoid(x):
    return 1.0 / (1.0 + jnp.exp2(x * NEG_LOG2_E))


def _softplus(x):
    return jnp.maximum(x, 0.0) + jnp.log1p(jnp.exp(-jnp.abs(x)))


def _one_minus_square_of_exp(log_a, a):
    y = 2.0 * log_a
    u = a * a
    near_zero = jnp.where(u == 1.0, -y, (1.0 - u) * y / jnp.log(u))
    return jnp.where(y < -0.5, 1.0 - u, near_zero)


def _pack_rows(w):
    return pltpu.bitcast(w.astype(jnp.bfloat16), jnp.uint32)


def _as_bf16(packed):
    return pltpu.bitcast(packed, jnp.bfloat16)


def _unit_rms(x):
    return x * lax.rsqrt(jnp.mean(x * x, axis=-1, keepdims=True) + RMS_EPS)


def _rmsnorm(x, g):
    return _unit_rms(x) * g


def _scan_rows(a, b, carry):
    rows, width = a.shape
    blocks = rows // SUBLANES
    a3 = a.reshape(blocks, SUBLANES, width)
    b3 = b.reshape(blocks, SUBLANES, width)
    row = lax.broadcasted_iota(jnp.int32, (blocks, SUBLANES, width), 1)
    shift = 1
    while shift < SUBLANES:
        keep = row >= shift
        a_prev = pltpu.roll(a3, shift, axis=1)
        b_prev = pltpu.roll(b3, shift, axis=1)
        b3 = jnp.where(keep, a3 * b_prev + b3, b3)
        a3 = jnp.where(keep, a3 * a_prev, a3)
        shift *= 2
    last = slice(SUBLANES - 1, SUBLANES)
    carry = jnp.broadcast_to(carry, (SUBLANES, width))
    out = []
    for k in range(blocks):
        out.append(a3[k] * carry + b3[k])
        carry = (jnp.broadcast_to(a3[k][last], (SUBLANES, width)) * carry
                 + jnp.broadcast_to(b3[k][last], (SUBLANES, width)))
    return jnp.concatenate(out, axis=0), carry[last]


def _layer_kernel(*refs, layer, n_tiles, own_steps, final_norm, emit_meta, pack_next):
    refs = list(refs)
    pack_own = own_steps > 0
    x_ref, meta_ref = refs.pop(0), refs.pop(0)
    win_ref = None if pack_own else refs.pop(0)
    caw_ref, cab_ref, wr_ref, br_ref, wi_ref, bi_ref, lam_ref, cbw_ref = [refs.pop(0) for _ in range(8)]
    wout_ref = None if pack_own else refs.pop(0)
    fg_ref = refs.pop(0)
    win_chunk, wout_chunk, g_chunk = ([refs.pop(0) for _ in range(3)] if pack_own or pack_next
                                      else (None, None, None))
    out_ref = refs.pop(0)
    meta_out_ref = refs.pop(0) if emit_meta else None
    next_win_out, next_wout_out = (refs.pop(0), refs.pop(0)) if pack_next else (None, None)
    xa_tail, cb_tail, h_carry, xa_tail0, cb_tail0, h_carry0, y_buf, y_meta = refs[:8]
    if pack_own:
        win_ref, wout_ref = refs[8:]
    chans = lam_ref.shape[-1]
    f32, bf16 = jnp.float32, jnp.bfloat16
    this = slice(layer, layer + 1)
    step = pl.program_id(0)
    tile_step = step - own_steps

    def mix(x, y_ref):
        rows = x.shape[0]
        blocks = rows // SUBLANES
        hn = _unit_rms(x).astype(bf16)

        def proj(split, c):
            col = split * chans + c * CHANNEL_BLOCK
            return jnp.dot(hn, _as_bf16(win_ref[:, col:col + CHANNEL_BLOCK]), preferred_element_type=f32)

        def as_blocks(v):
            return v.reshape(blocks, SUBLANES, v.shape[-1])

        def delayed(v3, tail, back):
            rolled = pltpu.roll(jnp.concatenate([tail[None], v3], axis=0), back, axis=1)
            row = lax.broadcasted_iota(jnp.int32, v3.shape, 1)
            return jnp.where(row < back, rolled[:-1], rolled[1:])

        def causal_conv(v3, tail, taps_ref, cs):
            n_taps = taps_ref.shape[0]
            acc = taps_ref[n_taps - 1:n_taps, cs] * v3
            for k in range(n_taps - 1):
                acc = acc + taps_ref[k:k + 1, cs] * delayed(v3, tail, n_taps - 1 - k)
            return acc

        for c in range(chans // CHANNEL_BLOCK):
            cs = slice(c * CHANNEL_BLOCK, (c + 1) * CHANNEL_BLOCK)
            ys = slice(chans + c * CHANNEL_BLOCK, chans + (c + 1) * CHANNEL_BLOCK)

            xa = as_blocks(proj(0, c))
            xc = causal_conv(xa, xa_tail[:, cs], caw_ref, cs) + cab_ref[this, cs]
            xa_tail[:, cs] = xa[blocks - 1]
            xcb = xc.reshape(rows, CHANNEL_BLOCK).astype(bf16)
            r_pre = jnp.dot(xcb, _as_bf16(wr_ref[c]), preferred_element_type=f32)
            i_pre = jnp.dot(xcb, _as_bf16(wi_ref[c]), preferred_element_type=f32)
            xc2 = xc.reshape(rows, CHANNEL_BLOCK)

            def recur(lo, hi, carry):
                r = _sigmoid(r_pre[lo:hi] + br_ref[this, cs])
                i = _sigmoid(i_pre[lo:hi] + bi_ref[this, cs])
                log_a = (-LRU_C) * r * _softplus(-lam_ref[this, cs])
                a = jnp.exp(log_a)
                b = jnp.sqrt(_one_minus_square_of_exp(log_a, a)) * (i * xc2[lo:hi])
                return _scan_rows(a, b, carry)

            half = rows // 2 if rows % (2 * SUBLANES) == 0 and rows > BF16_TILE_ROWS else rows
            h, carry = recur(0, half, h_carry[:, cs])
            gate_c = proj(3, c)
            if half < rows:
                h_hi, carry = recur(half, rows, carry)
                h = jnp.concatenate([h, h_hi], axis=0)
            h_carry[:, cs] = carry
            ga = proj(1, c)
            y_ref[:, cs] = (h * (ga * _sigmoid(ga))).astype(bf16)

            gate_b = proj(2, c)
            cb = as_blocks(gate_c * proj(4, c))
            cv = causal_conv(cb, cb_tail[:, cs], cbw_ref, cs).reshape(rows, CHANNEL_BLOCK)
            cb_tail[:, cs] = cb[blocks - 1]
            gb = proj(5, c)
            y_ref[:, ys] = (gate_b * cv * (gb * _sigmoid(gb))).astype(bf16)

    def project_out(x, y_ref):
        return x + jnp.dot(y_ref[...], _as_bf16(wout_ref[...]), preferred_element_type=f32)

    if pack_own:
        @pl.when(step < own_steps)
        def _():
            for chunk, dst in ((win_chunk[...] * g_chunk[...], win_ref), (wout_chunk[...], wout_ref)):
                rows = chunk.shape[0] // 2
                dst[pl.ds(pl.multiple_of(step * rows, rows), rows), :] = _pack_rows(chunk)

    @pl.when(tile_step == 0)
    def _():
        xa_tail[...] = jnp.zeros_like(xa_tail)
        cb_tail[...] = jnp.zeros_like(cb_tail)
        h_carry[...] = jnp.zeros_like(h_carry)
        meta = meta_ref[...]
        mix(meta, y_meta)
        if emit_meta:
            meta_out_ref[...] = project_out(meta, y_meta)
        xa_tail0[...] = xa_tail[...]
        cb_tail0[...] = cb_tail[...]
        h_carry0[...] = h_carry[...]

    @pl.when((tile_step >= 0) & (tile_step % n_tiles == 0))
    def _():
        xa_tail[...] = xa_tail0[...]
        cb_tail[...] = cb_tail0[...]
        h_carry[...] = h_carry0[...]

    def tile():
        x = x_ref[...]
        mix(x, y_buf)
        out = project_out(x, y_buf)
        if final_norm:
            out = _rmsnorm(out, fg_ref[...])
        out_ref[...] = out
        if pack_next:
            next_win_out[...] = _pack_rows(win_chunk[...] * g_chunk[...])
            next_wout_out[...] = _pack_rows(wout_chunk[...])

    if pack_own:
        pl.when(tile_step >= 0)(tile)
    else:
        tile()


def _resident(shape, lead=()):
    block = (None,) * len(lead) + tuple(shape[len(lead):])
    index = tuple(lead) + (0,) * (len(shape) - len(lead))
    return pl.BlockSpec(block, lambda s: index, pipeline_mode=pl.Buffered(1))


def _layer_call(x, meta, p, layer, *, final_norm, emit_meta, pack_own, pack_next, name):
    batch, length, d_model = x.shape
    meta_rows = meta.shape[0]
    chans = p["lam"].shape[-1]
    rows = MAIN_TILE_ROWS
    n_tiles = length // rows
    tile_steps = batch * n_tiles
    own_steps = tile_steps if pack_own else 0
    assert length % rows == 0 and chans % CHANNEL_BLOCK == 0
    assert rows % BF16_TILE_ROWS == 0 and meta_rows % BF16_TILE_ROWS == 0
    f32 = jnp.float32

    def tile_index(s):
        t = jnp.maximum(s - own_steps, 0)
        return (t // n_tiles, t % n_tiles, 0)

    tile = pl.BlockSpec((None, rows, d_model), tile_index)
    args = [x, meta]
    in_specs = [tile, _resident(meta.shape)]
    if not pack_own:
        args.append(p["win"][layer])
        in_specs.append(_resident(p["win"][layer].shape))
    args += [p["caw"], p["cab"], p["gates"], p["br"], p["gates"], p["bi"], p["lam"], p["cbw"]]
    in_specs += [_resident(p["caw"].shape, (layer,)), _resident(p["cab"].shape),
                 _resident(p["gates"].shape, (0, layer)), _resident(p["br"].shape),
                 _resident(p["gates"].shape, (1, layer)), _resident(p["bi"].shape),
                 _resident(p["lam"].shape), _resident(p["cbw"].shape, (layer,))]
    if not pack_own:
        args.append(p["wout"][layer])
        in_specs.append(_resident(p["wout"][layer].shape))
    args.append(p["fg"])
    in_specs.append(_resident(p["fg"].shape))

    out_shape = [jax.ShapeDtypeStruct(x.shape, f32)]
    out_specs = [tile]
    if emit_meta:
        out_shape.append(jax.ShapeDtypeStruct(meta.shape, f32))
        out_specs.append(pl.BlockSpec(meta.shape, lambda s: (0, 0)))
    scratch = (
        [pltpu.VMEM((TAIL_ROWS, chans), f32),
         pltpu.VMEM((TAIL_ROWS, chans), f32),
         pltpu.VMEM((1, chans), f32)] * 2
        + [pltpu.VMEM((rows, 2 * chans), jnp.bfloat16),
           pltpu.VMEM((meta_rows, 2 * chans), jnp.bfloat16)])
    if pack_own or pack_next:
        for w in (p["w_in"], p["w_out"]):
            w_rows, w_cols = w.shape[1:]
            chunk = w_rows // tile_steps
            assert w_rows % tile_steps == 0 and chunk % BF16_TILE_ROWS == 0
            args.append(w)
            if pack_own and pack_next:
                chunk_index = lambda s: (layer + s // own_steps, s % own_steps, 0)
            elif pack_own:
                chunk_index = lambda s: (layer, jnp.minimum(s, own_steps - 1), 0)
            else:
                chunk_index = lambda s: (layer + 1, s, 0)
            in_specs.append(pl.BlockSpec((None, chunk, w_cols), chunk_index))
            if pack_next:
                out_shape.append(jax.ShapeDtypeStruct((w_rows // 2, w_cols), jnp.uint32))
                out_specs.append(pl.BlockSpec((chunk // 2, w_cols),
                                              lambda s: (jnp.maximum(s - own_steps, 0), 0)))
            if pack_own:
                scratch.append(pltpu.VMEM((w_rows // 2, w_cols), jnp.uint32))
        args.append(p["gcol"])
        in_specs.append(pl.BlockSpec((None, p["w_in"].shape[1] // tile_steps, 1), chunk_index))
    kern = functools.partial(_layer_kernel, layer=layer, n_tiles=n_tiles, own_steps=own_steps,
                             final_norm=final_norm, emit_meta=emit_meta, pack_next=pack_next)
    return pl.pallas_call(
        kern,
        grid=(own_steps + tile_steps,),
        in_specs=in_specs,
        out_specs=out_specs,
        out_shape=out_shape,
        scratch_shapes=scratch,
        compiler_params=pltpu.CompilerParams(
            dimension_semantics=("arbitrary",),
            vmem_limit_bytes=VMEM_LIMIT_BYTES),
        name=name,
    )(*args)


def _pack_gates_kernel(wr_ref, wi_ref, o_ref):
    depth, heads, d, _ = wr_ref.shape
    group = CHANNEL_BLOCK // d
    zero = jnp.zeros((d, d), jnp.float32)
    for which, ref in enumerate((wr_ref, wi_ref)):
        for layer in range(depth):
            for n in range(heads // group):
                tile = jnp.concatenate(
                    [jnp.concatenate([ref[layer, n * group + g] if h == g else zero
                                      for h in range(group)], axis=1) for g in range(group)], axis=0)
                o_ref[which, layer, n] = _pack_rows(tile)


def _pack_gates(lru_wr, lru_wi):
    depth, heads, d, _ = lru_wr.shape
    assert CHANNEL_BLOCK % d == 0 and heads % (CHANNEL_BLOCK // d) == 0
    tiles = heads * d // CHANNEL_BLOCK
    return pl.pallas_call(
        _pack_gates_kernel,
        out_shape=jax.ShapeDtypeStruct((2, depth, tiles, CHANNEL_BLOCK // 2, CHANNEL_BLOCK), jnp.uint32),
        name="pack_gates",
    )(lru_wr, lru_wi)


def kernel(x, meta, norm_g, w_in, conv_a_w, conv_a_b, lru_wr, lru_br, lru_wi, lru_bi, lru_lambda,
           conv_b_w, w_out, final_g):
    depth = norm_g.shape[0]
    assert w_in.shape[-1] == N_SPLITS * lru_lambda.shape[-1]
    p = dict(gcol=norm_g[:, :, None], caw=conv_a_w, cab=conv_a_b, gates=_pack_gates(lru_wr, lru_wi),
             br=lru_br, bi=lru_bi, lam=lru_lambda, cbw=conv_b_w, fg=final_g[None],
             w_in=w_in, w_out=w_out, win=[None], wout=[None])
    h = x
    for layer in range(depth):
        last = layer == depth - 1
        h, *more = _layer_call(h, meta, p, layer, final_norm=last, emit_meta=not last,
                               pack_own=layer == 0, pack_next=not last, name=f"layer{layer}")
        if not last:
            meta, next_win, next_wout = more
            p["win"].append(next_win)
            p["wout"].append(next_wout)
    return h
```

```python
import functools

import jax
import jax.numpy as jnp
from jax import lax
from jax.experimental import pallas as pl
from jax.experimental.pallas import tpu as pltpu

LRU_C = 8.0
RMS_EPS = 1e-6
N_SPLITS = 6
NEG_LOG2_E = -1.4426950408889634

SUBLANES = 8
BF16_TILE_ROWS = 16
CHANNEL_BLOCK = 256
TAIL_ROWS = SUBLANES
MAIN_TILE_ROWS = 512
VMEM_LIMIT_BYTES = 63 * 1024 * 1024


def _sigmoid(x):
    return 1.0 / (1.0 + jnp.exp2(x * NEG_LOG2_E))


def _softplus(x):
    return jnp.maximum(x, 0.0) + jnp.log1p(jnp.exp(-jnp.abs(x)))


def _one_minus_square_of_exp(log_a, a):
    y = 2.0 * log_a
    u = a * a
    near_zero = jnp.where(u == 1.0, -y, (1.0 - u) * y / jnp.log(u))
    return jnp.where(y < -0.5, 1.0 - u, near_zero)


def _pack_rows(w):
    return pltpu.bitcast(w.astype(jnp.bfloat16), jnp.uint32)


def _as_bf16(packed):
    return pltpu.bitcast(packed, jnp.bfloat16)


def _rmsnorm(x, g):
    return x * lax.rsqrt(jnp.mean(x * x, axis=-1, keepdims=True) + RMS_EPS) * g


def _scan_rows(a, b, carry):
    rows, width = a.shape
    blocks = rows // SUBLANES
    a3 = a.reshape(blocks, SUBLANES, width)
    b3 = b.reshape(blocks, SUBLANES, width)
    row = lax.broadcasted_iota(jnp.int32, (blocks, SUBLANES, width), 1)
    shift = 1
    while shift < SUBLANES:
        keep = row >= shift
        a_prev = pltpu.roll(a3, shift, axis=1)
        b_prev = pltpu.roll(b3, shift, axis=1)
        b3 = jnp.where(keep, a3 * b_prev + b3, b3)
        a3 = jnp.where(keep, a3 * a_prev, a3)
        shift *= 2
    last = slice(SUBLANES - 1, SUBLANES)
    carry = jnp.broadcast_to(carry, (SUBLANES, width))
    out = []
    for k in range(blocks):
        out.append(a3[k] * carry + b3[k])
        carry = (jnp.broadcast_to(a3[k][last], (SUBLANES, width)) * carry
                 + jnp.broadcast_to(b3[k][last], (SUBLANES, width)))
    return jnp.concatenate(out, axis=0), carry[last]


def _layer_kernel(*refs, layer, n_tiles, own_steps, final_norm, emit_meta, pack_next):
    refs = list(refs)
    pack_own = own_steps > 0
    x_ref, meta_ref, g_ref = refs.pop(0), refs.pop(0), refs.pop(0)
    win_ref = None if pack_own else refs.pop(0)
    caw_ref, cab_ref, wr_ref, br_ref, wi_ref, bi_ref, lam_ref, cbw_ref = [refs.pop(0) for _ in range(8)]
    wout_ref = None if pack_own else refs.pop(0)
    fg_ref = refs.pop(0)
    win_chunk, wout_chunk = (refs.pop(0), refs.pop(0)) if pack_own or pack_next else (None, None)
    out_ref = refs.pop(0)
    meta_out_ref = refs.pop(0) if emit_meta else None
    next_win_out, next_wout_out = (refs.pop(0), refs.pop(0)) if pack_next else (None, None)
    xa_tail, cb_tail, h_carry, xa_tail0, cb_tail0, h_carry0, y_buf, y_meta = refs[:8]
    if pack_own:
        win_ref, wout_ref = refs[8:]
    chans = lam_ref.shape[-1]
    f32, bf16 = jnp.float32, jnp.bfloat16
    this = slice(layer, layer + 1)
    step = pl.program_id(0)
    tile_step = step - own_steps

    def mix(x, y_ref):
        rows = x.shape[0]
        blocks = rows // SUBLANES
        hn = _rmsnorm(x, g_ref[this, :]).astype(bf16)

        def proj(split, c):
            col = split * chans + c * CHANNEL_BLOCK
            return jnp.dot(hn, _as_bf16(win_ref[:, col:col + CHANNEL_BLOCK]), preferred_element_type=f32)

        def as_blocks(v):
            return v.reshape(blocks, SUBLANES, v.shape[-1])

        def delayed(v3, tail, back):
            rolled = pltpu.roll(jnp.concatenate([tail[None], v3], axis=0), back, axis=1)
            row = lax.broadcasted_iota(jnp.int32, v3.shape, 1)
            return jnp.where(row < back, rolled[:-1], rolled[1:])

        def causal_conv(v3, tail, taps_ref, cs):
            n_taps = taps_ref.shape[0]
            acc = taps_ref[n_taps - 1:n_taps, cs] * v3
            for k in range(n_taps - 1):
                acc = acc + taps_ref[k:k + 1, cs] * delayed(v3, tail, n_taps - 1 - k)
            return acc

        for c in range(chans // CHANNEL_BLOCK):
            cs = slice(c * CHANNEL_BLOCK, (c + 1) * CHANNEL_BLOCK)
            ys = slice(chans + c * CHANNEL_BLOCK, chans + (c + 1) * CHANNEL_BLOCK)

            xa = as_blocks(proj(0, c))
            xc = causal_conv(xa, xa_tail[:, cs], caw_ref, cs) + cab_ref[this, cs]
            xa_tail[:, cs] = xa[blocks - 1]
            xcb = xc.reshape(rows, CHANNEL_BLOCK).astype(bf16)
            r_pre = jnp.dot(xcb, _as_bf16(wr_ref[c]), preferred_element_type=f32)
            i_pre = jnp.dot(xcb, _as_bf16(wi_ref[c]), preferred_element_type=f32)
            xc2 = xc.reshape(rows, CHANNEL_BLOCK)

            def recur(lo, hi, carry):
                r = _sigmoid(r_pre[lo:hi] + br_ref[this, cs])
                i = _sigmoid(i_pre[lo:hi] + bi_ref[this, cs])
                log_a = r * ((-LRU_C) * _softplus(-lam_ref[this, cs]))
                a = jnp.exp(log_a)
                b = jnp.sqrt(_one_minus_square_of_exp(log_a, a)) * (i * xc2[lo:hi])
                return _scan_rows(a, b, carry)

            half = rows // 2 if rows % (2 * SUBLANES) == 0 and rows > BF16_TILE_ROWS else rows
            h, carry = recur(0, half, h_carry[:, cs])
            gate_c = proj(3, c)
            if half < rows:
                h_hi, carry = recur(half, rows, carry)
                h = jnp.concatenate([h, h_hi], axis=0)
            h_carry[:, cs] = carry
            ga = proj(1, c)
            y_ref[:, cs] = (h * (ga * _sigmoid(ga))).astype(bf16)

            gate_b = proj(2, c)
            cb = as_blocks(gate_c * proj(4, c))
            cv = causal_conv(cb, cb_tail[:, cs], cbw_ref, cs).reshape(rows, CHANNEL_BLOCK)
            cb_tail[:, cs] = cb[blocks - 1]
            gb = proj(5, c)
            y_ref[:, ys] = (gate_b * cv * (gb * _sigmoid(gb))).astype(bf16)

    def project_out(x, y_ref):
        return x + jnp.dot(y_ref[...], _as_bf16(wout_ref[...]), preferred_element_type=f32)

    if pack_own:
        @pl.when(step < own_steps)
        def _():
            for chunk, dst in ((win_chunk, win_ref), (wout_chunk, wout_ref)):
                rows = chunk.shape[0] // 2
                dst[pl.ds(pl.multiple_of(step * rows, rows), rows), :] = _pack_rows(chunk[...])

    @pl.when(tile_step == 0)
    def _():
        xa_tail[...] = jnp.zeros_like(xa_tail)
        cb_tail[...] = jnp.zeros_like(cb_tail)
        h_carry[...] = jnp.zeros_like(h_carry)
        meta = meta_ref[...]
        mix(meta, y_meta)
        if emit_meta:
            meta_out_ref[...] = project_out(meta, y_meta)
        xa_tail0[...] = xa_tail[...]
        cb_tail0[...] = cb_tail[...]
        h_carry0[...] = h_carry[...]

    @pl.when((tile_step >= 0) & (tile_step % n_tiles == 0))
    def _():
        xa_tail[...] = xa_tail0[...]
        cb_tail[...] = cb_tail0[...]
        h_carry[...] = h_carry0[...]

    def tile():
        x = x_ref[...]
        mix(x, y_buf)
        out = project_out(x, y_buf)
        if final_norm:
            out = _rmsnorm(out, fg_ref[...])
        out_ref[...] = out
        if pack_next:
            next_win_out[...] = _pack_rows(win_chunk[...])
            next_wout_out[...] = _pack_rows(wout_chunk[...])

    if pack_own:
        pl.when(tile_step >= 0)(tile)
    else:
        tile()


def _resident(shape, lead=()):
    block = (None,) * len(lead) + tuple(shape[len(lead):])
    index = tuple(lead) + (0,) * (len(shape) - len(lead))
    return pl.BlockSpec(block, lambda s: index, pipeline_mode=pl.Buffered(1))


def _layer_call(x, meta, p, layer, *, final_norm, emit_meta, pack_own, pack_next, name):
    batch, length, d_model = x.shape
    meta_rows = meta.shape[0]
    chans = p["lam"].shape[-1]
    rows = MAIN_TILE_ROWS
    n_tiles = length // rows
    tile_steps = batch * n_tiles
    own_steps = tile_steps if pack_own else 0
    assert length % rows == 0 and chans % CHANNEL_BLOCK == 0
    assert rows % BF16_TILE_ROWS == 0 and meta_rows % BF16_TILE_ROWS == 0
    f32 = jnp.float32

    def tile_index(s):
        t = jnp.maximum(s - own_steps, 0)
        return (t // n_tiles, t % n_tiles, 0)

    tile = pl.BlockSpec((None, rows, d_model), tile_index)
    args = [x, meta, p["g"]]
    in_specs = [tile, _resident(meta.shape), _resident(p["g"].shape)]
    if not pack_own:
        args.append(p["win"][layer])
        in_specs.append(_resident(p["win"][layer].shape))
    args += [p["caw"], p["cab"], p["gates"], p["br"], p["gates"], p["bi"], p["lam"], p["cbw"]]
    in_specs += [_resident(p["caw"].shape, (layer,)), _resident(p["cab"].shape),
                 _resident(p["gates"].shape, (0, layer)), _resident(p["br"].shape),
                 _resident(p["gates"].shape, (1, layer)), _resident(p["bi"].shape),
                 _resident(p["lam"].shape), _resident(p["cbw"].shape, (layer,))]
    if not pack_own:
        args.append(p["wout"][layer])
        in_specs.append(_resident(p["wout"][layer].shape))
    args.append(p["fg"])
    in_specs.append(_resident(p["fg"].shape))

    out_shape = [jax.ShapeDtypeStruct(x.shape, f32)]
    out_specs = [tile]
    if emit_meta:
        out_shape.append(jax.ShapeDtypeStruct(meta.shape, f32))
        out_specs.append(pl.BlockSpec(meta.shape, lambda s: (0, 0)))
    scratch = (
        [pltpu.VMEM((TAIL_ROWS, chans), f32),
         pltpu.VMEM((TAIL_ROWS, chans), f32),
         pltpu.VMEM((1, chans), f32)] * 2
        + [pltpu.VMEM((rows, 2 * chans), jnp.bfloat16),
           pltpu.VMEM((meta_rows, 2 * chans), jnp.bfloat16)])
    if pack_own or pack_next:
        for w in (p["w_in"], p["w_out"]):
            w_rows, w_cols = w.shape[1:]
            chunk = w_rows // tile_steps
            assert w_rows % tile_steps == 0 and chunk % BF16_TILE_ROWS == 0
            args.append(w)
            if pack_own and pack_next:
                chunk_index = lambda s: (layer + s // own_steps, s % own_steps, 0)
            elif pack_own:
                chunk_index = lambda s: (layer, jnp.minimum(s, own_steps - 1), 0)
            else:
                chunk_index = lambda s: (layer + 1, s, 0)
            in_specs.append(pl.BlockSpec((None, chunk, w_cols), chunk_index))
            if pack_next:
                out_shape.append(jax.ShapeDtypeStruct((w_rows // 2, w_cols), jnp.uint32))
                out_specs.append(pl.BlockSpec((chunk // 2, w_cols),
                                              lambda s: (jnp.maximum(s - own_steps, 0), 0)))
            if pack_own:
                scratch.append(pltpu.VMEM((w_rows // 2, w_cols), jnp.uint32))
    kern = functools.partial(_layer_kernel, layer=layer, n_tiles=n_tiles, own_steps=own_steps,
                             final_norm=final_norm, emit_meta=emit_meta, pack_next=pack_next)
    return pl.pallas_call(
        kern,
        grid=(own_steps + tile_steps,),
        in_specs=in_specs,
        out_specs=out_specs,
        out_shape=out_shape,
        scratch_shapes=scratch,
        compiler_params=pltpu.CompilerParams(
            dimension_semantics=("arbitrary",),
            vmem_limit_bytes=VMEM_LIMIT_BYTES),
        name=name,
    )(*args)


def _pack_gates_kernel(wr_ref, wi_ref, o_ref):
    depth, heads, d, _ = wr_ref.shape
    group = CHANNEL_BLOCK // d
    zero = jnp.zeros((d, d), jnp.float32)
    for which, ref in enumerate((wr_ref, wi_ref)):
        for layer in range(depth):
            for n in range(heads // group):
                tile = jnp.concatenate(
                    [jnp.concatenate([ref[layer, n * group + g] if h == g else zero
                                      for h in range(group)], axis=1) for g in range(group)], axis=0)
                o_ref[which, layer, n] = _pack_rows(tile)


def _pack_gates(lru_wr, lru_wi):
    depth, heads, d, _ = lru_wr.shape
    assert CHANNEL_BLOCK % d == 0 and heads % (CHANNEL_BLOCK // d) == 0
    tiles = heads * d // CHANNEL_BLOCK
    return pl.pallas_call(
        _pack_gates_kernel,
        out_shape=jax.ShapeDtypeStruct((2, depth, tiles, CHANNEL_BLOCK // 2, CHANNEL_BLOCK), jnp.uint32),
        name="pack_gates",
    )(lru_wr, lru_wi)


def kernel(x, meta, norm_g, w_in, conv_a_w, conv_a_b, lru_wr, lru_br, lru_wi, lru_bi, lru_lambda,
           conv_b_w, w_out, final_g):
    depth = norm_g.shape[0]
    assert w_in.shape[-1] == N_SPLITS * lru_lambda.shape[-1]
    p = dict(g=norm_g, caw=conv_a_w, cab=conv_a_b, gates=_pack_gates(lru_wr, lru_wi),
             br=lru_br, bi=lru_bi, lam=lru_lambda, cbw=conv_b_w, fg=final_g[None],
             w_in=w_in, w_out=w_out, win=[None], wout=[None])
    h = x
    for layer in range(depth):
        last = layer == depth - 1
        h, *more = _layer_call(h, meta, p, layer, final_norm=last, emit_meta=not last,
                               pack_own=layer == 0, pack_next=not last, name=f"layer{layer}")
        if not last:
            meta, next_win, next_wout = more
            p["win"].append(next_win)
            p["wout"].append(next_wout)
    return h
```
